```python
import math
import jax, jax.numpy as jnp
from jax import lax
import numpy as np

D_MODEL = 2048
BATCH = 1
SEQ = 16384
DEPTH = 4
DEC_BATCH = 8
DEC_SEQ = 32
PAST_LEN = 2048

CHUNK = 64
Q_BLOCK = 128
ROPE_THETA = 10000.0
EPS = 1e-6

D_ATTN = D_MODEL // 2
D_SSD = D_MODEL // 4
D_CONV = D_MODEL // 4
N_HEADS_A = 8
HEAD_DIM_A = D_ATTN // (2 * N_HEADS_A)
SSD_HEAD_DIM = 64
N_HEADS_B = D_SSD // SSD_HEAD_DIM
SSD_GROUPS = 2
SSD_STATE = 128
SSD_CONV = 4
CONV_DIM_B = D_SSD + 2 * SSD_GROUPS * SSD_STATE
CONF_KERNEL = 31
D_FF = 5632
N_EXPERTS = 8
TOP_K = 2
D_FF_EXPERT = 7168

IN_COLS = 3 * D_ATTN + D_SSD + CONV_DIM_B + N_HEADS_B + 2 * D_CONV
MOD_COLS = 6 * D_MODEL

kernel_name = 'hybrid_streaming_encoder_step'


def _rmsnorm(x, g):
    xf = x.astype(jnp.float32)
    y = xf * lax.rsqrt(jnp.mean(xf * xf, axis=-1, keepdims=True) + EPS)
    return (y * g.astype(jnp.float32)).astype(x.dtype)


def _layernorm(x, g, b):
    xf = x.astype(jnp.float32)
    mu = jnp.mean(xf, axis=-1, keepdims=True)
    var = jnp.mean(jnp.square(xf - mu), axis=-1, keepdims=True)
    y = (xf - mu) * lax.rsqrt(var + EPS)
    return (y * g.astype(jnp.float32) + b.astype(jnp.float32)).astype(x.dtype)


def _rope(x, pos):
    half = x.shape[-1] // 2
    inv = ROPE_THETA ** (-jnp.arange(half, dtype=jnp.float32) / half)
    ang = pos.astype(jnp.float32)[:, None] * inv[None, :]
    cos = jnp.cos(ang)[None, :, None, None, :]
    sin = jnp.sin(ang)[None, :, None, None, :]
    xf = x.astype(jnp.float32)
    x1, x2 = xf[..., :half], xf[..., half:]
    return jnp.concatenate([x1 * cos - x2 * sin, x2 * cos + x1 * sin], axis=-1).astype(x.dtype)


def _causal_dwconv(x, buf, w, b):
    xp = jnp.concatenate([buf, x], axis=1)
    y = lax.conv_general_dilated(xp, w[:, None, :], window_strides=(1,), padding='VALID',
                                 dimension_numbers=('NWC', 'WIO', 'NWC'), feature_group_count=x.shape[-1])
    return y + b, xp[:, xp.shape[1] - buf.shape[1]:]


def _diff_attention(q, k, v, q_pos, k_pos, lam):
    b, lq = q.shape[:2]
    qb = min(Q_BLOCK, lq)
    nb = lq // qb
    qs = jnp.moveaxis(q.reshape(b, nb, qb, *q.shape[2:]), 1, 0)
    ps = q_pos.reshape(nb, qb)
    k_chunk = k_pos // CHUNK
    scale = HEAD_DIM_A ** -0.5

    def one_block(args):
        qi, pi = args
        s = jnp.einsum('bqhmd,bkhmd->bhmqk', qi, k).astype(jnp.float32) * scale
        mask = k_chunk[None, :] <= (pi // CHUNK)[:, None]
        p = jax.nn.softmax(jnp.where(mask, s, -jnp.inf), axis=-1)
        p = p[:, :, 0] - lam * p[:, :, 1]
        return jnp.einsum('bhqk,bkhe->bqhe', p.astype(v.dtype), v)

    o = lax.map(one_block, (qs, ps))
    return jnp.moveaxis(o, 0, 1).reshape(b, lq, q.shape[2], v.shape[-1])


def _ssd(x, dt, a, bm, cm, h0):
    b, L, H, P = x.shape
    q = min(CHUNK, L)
    nc = L // q
    rep = H // bm.shape[2]
    xf = x.astype(jnp.float32).reshape(b, nc, q, H, P)
    dtc = dt.reshape(b, nc, q, H)
    bh = jnp.repeat(bm.astype(jnp.float32), rep, axis=2).reshape(b, nc, q, H, -1)
    ch = jnp.repeat(cm.astype(jnp.float32), rep, axis=2).reshape(b, nc, q, H, -1)
    acum = jnp.cumsum(dtc * a, axis=2)
    causal = jnp.tril(jnp.ones((q, q), dtype=bool))[None, None, :, :, None]
    seg = acum[:, :, :, None, :] - acum[:, :, None, :, :]
    decay = jnp.exp(jnp.where(causal, seg, -jnp.inf))
    xdt = xf * dtc[..., None]
    scores = jnp.einsum('bcihn,bcjhn->bcijh', ch, bh) * decay
    y_diag = jnp.einsum('bcijh,bcjhp->bcihp', scores, xdt)
    to_end = jnp.exp(acum[:, :, -1:, :] - acum)
    states = jnp.einsum('bcjhn,bcjhp->bchpn', bh * to_end[..., None], xdt)
    chunk_decay = jnp.exp(acum[:, :, -1, :])

    def step(h, inp):
        st, dec = inp
        return dec[:, :, None, None] * h + st, h

    h_last, h_start = lax.scan(step, h0.astype(jnp.float32),
                               (jnp.moveaxis(states, 1, 0), jnp.moveaxis(chunk_decay, 1, 0)))
    h_start = jnp.moveaxis(h_start, 0, 1)
    y_off = jnp.einsum('bcihn,bchpn->bcihp', ch * jnp.exp(acum)[..., None], h_start)
    y = (y_diag + y_off).reshape(b, L, H, P)
    return y.astype(x.dtype), h_last.astype(h0.dtype)


def _lambda_init(layer):
    return 0.8 - 0.6 * math.exp(-0.3 * layer)


def _adaln(c, w, b):
    mod = jax.nn.silu(c) @ w + b
    return jnp.split(mod, 6, axis=-1)


def _modulate(x, g, shift, scale):
    return _rmsnorm(x, g) * (1 + scale[:, None, :]) + shift[:, None, :]


def _swiglu(h, w1, w3, w2):
    return (jax.nn.silu(h @ w1) * (h @ w3)) @ w2


def _moe(h, router, w1, w3, w2):
    logits = (h @ router).astype(jnp.float32)
    top_v, top_i = lax.top_k(logits, TOP_K)
    gates = jax.nn.softmax(top_v, axis=-1)
    dense_gate = jnp.sum(jax.nn.one_hot(top_i, N_EXPERTS, dtype=jnp.float32) * gates[..., None], axis=-2).astype(h.dtype)
    out = jnp.zeros_like(h)
    for e in range(N_EXPERTS):
        out = out + dense_gate[..., e:e + 1] * _swiglu(h, w1[e], w3[e], w2[e])
    return out


def _token_mixers(h, pos, past, p, lambda_init):
    k_past, v_past, ssm0, buf_b, buf_c = past
    b, L, _ = h.shape
    proj = h @ p['w_in']
    cuts = [D_ATTN, 2 * D_ATTN, 3 * D_ATTN, 3 * D_ATTN + D_SSD, 3 * D_ATTN + D_SSD + CONV_DIM_B,
            3 * D_ATTN + D_SSD + CONV_DIM_B + N_HEADS_B]
    q, k, v, z, xbc, dtr, conf = jnp.split(proj, cuts, axis=-1)

    q = _rope(q.reshape(b, L, N_HEADS_A, 2, HEAD_DIM_A), pos)
    k = _rope(k.reshape(b, L, N_HEADS_A, 2, HEAD_DIM_A), pos)
    v = v.reshape(b, L, N_HEADS_A, 2 * HEAD_DIM_A)
    if k_past is None:
        k_all, v_all, k_pos = k, v, pos
    else:
        k_all = jnp.concatenate([k_past, k], axis=1)
        v_all = jnp.concatenate([v_past, v], axis=1)
        k_pos = jnp.concatenate([jnp.arange(k_past.shape[1], dtype=jnp.int32), pos])
    lp = p['attn_lambda'].astype(jnp.float32)
    lam = jnp.exp(jnp.sum(lp[0] * lp[1])) - jnp.exp(jnp.sum(lp[2] * lp[3])) + lambda_init
    o = _diff_attention(q, k_all, v_all, pos, k_pos, lam)
    y_a = (_rmsnorm(o, p['attn_subln']) * (1 - lambda_init)).reshape(b, L, D_ATTN)

    xbc, new_buf_b = _causal_dwconv(xbc, buf_b, p['ssd_conv_w'], p['ssd_conv_b'])
    xbc = jax.nn.silu(xbc)
    xs, bm, cm = jnp.split(xbc, [D_SSD, D_SSD + SSD_GROUPS * SSD_STATE], axis=-1)
    xs = xs.reshape(b, L, N_HEADS_B, SSD_HEAD_DIM)
    bm = bm.reshape(b, L, SSD_GROUPS, SSD_STATE)
    cm = cm.reshape(b, L, SSD_GROUPS, SSD_STATE)
    dt = jax.nn.softplus(dtr.astype(jnp.float32) + p['ssd_dt_bias'].astype(jnp.float32))
    a = -jnp.exp(p['ssd_a_log'].astype(jnp.float32))
    ys, ssm_new = _ssd(xs, dt, a, bm, cm, ssm0)
    ys = ys + p['ssd_d'][:, None] * xs
    y_b = _rmsnorm(ys.reshape(b, L, D_SSD) * jax.nn.silu(z), p['ssd_norm'])

    ga, gg = jnp.split(conf, 2, axis=-1)
    u = ga * jax.nn.sigmoid(gg)
    u, new_buf_c = _causal_dwconv(u, buf_c, p['conf_dw_w'], p['conf_dw_b'])
    y_c = jax.nn.silu(_layernorm(u, p['conf_ln_g'], p['conf_ln_b']))

    out = jnp.concatenate([y_a, y_b, y_c], axis=-1) @ p['w_out']
    return out, (k, v, ssm_new, new_buf_b, new_buf_c)


def _layer(x, c, pos, past, p, layer):
    sh1, sc1, g1, sh2, sc2, g2 = _adaln(c, p['w_ada'], p['b_ada'])
    h = _modulate(x, p['norm_mix'], sh1, sc1)
    mix, new_state = _token_mixers(h, pos, past, p, _lambda_init(layer))
    x = x + g1[:, None, :] * mix
    h = _modulate(x, p['norm_ffn'], sh2, sc2)
    if layer % 2 == 0:
        f = _swiglu(h, p['ffn_w1'], p['ffn_w3'], p['ffn_w2'])
    else:
        f = _moe(h, p['moe_router'], p['moe_w1'], p['moe_w3'], p['moe_w2'])
    x = x + g2[:, None, :] * f
    return x, new_state


def setup_inputs(seed: int = 0) -> dict:
    key = jax.random.key(seed)
    ks = iter(jax.random.split(key, 40))

    def nrm(shape, scale):
        return scale * jax.random.normal(next(ks), shape, dtype=jnp.float32)

    def gain(shape):
        return 1.0 + nrm(shape, 0.02)

    n_dense = (DEPTH + 1) // 2
    n_moe = DEPTH // 2
    d_in = D_MODEL ** -0.5
    inp = {
        'x_prompt': nrm((BATCH, SEQ, D_MODEL), 1.0),
        'x_sample': nrm((DEC_BATCH, DEC_SEQ, D_MODEL), 1.0),
        'c_prompt': nrm((BATCH, D_MODEL), 1.0),
        'c_sample': nrm((DEC_BATCH, D_MODEL), 1.0),
        'cache_k': nrm((DEPTH, DEC_BATCH, PAST_LEN, N_HEADS_A, 2, HEAD_DIM_A), 1.0),
        'cache_v': nrm((DEPTH, DEC_BATCH, PAST_LEN, N_HEADS_A, 2 * HEAD_DIM_A), 1.0),
        'state_ssm': nrm((DEPTH, DEC_BATCH, N_HEADS_B, SSD_HEAD_DIM, SSD_STATE), 0.5),
        'state_conv_ssd': nrm((DEPTH, DEC_BATCH, SSD_CONV - 1, CONV_DIM_B), 1.0),
        'state_conv_conf': nrm((DEPTH, DEC_BATCH, CONF_KERNEL - 1, D_CONV), 1.0),
        'w_ada': nrm((DEPTH, D_MODEL, MOD_COLS), 0.5 * d_in),
        'b_ada': nrm((DEPTH, MOD_COLS), 0.02),
        'norm_mix': gain((DEPTH, D_MODEL)),
        'norm_ffn': gain((DEPTH, D_MODEL)),
        'w_in': nrm((DEPTH, D_MODEL, IN_COLS), d_in),
        'w_out': nrm((DEPTH, D_MODEL, D_MODEL), d_in),
        'attn_lambda': nrm((DEPTH, 4, HEAD_DIM_A), 0.1),
        'attn_subln': gain((DEPTH, 2 * HEAD_DIM_A)),
        'ssd_conv_w': nrm((DEPTH, SSD_CONV, CONV_DIM_B), SSD_CONV ** -0.5),
        'ssd_conv_b': nrm((DEPTH, CONV_DIM_B), 0.02),
    }
    dt0 = jnp.exp(jax.random.uniform(next(ks), (DEPTH, N_HEADS_B), jnp.float32, math.log(1e-3), math.log(1e-1)))
    inp['ssd_dt_bias'] = dt0 + jnp.log(-jnp.expm1(-dt0))
    inp['ssd_a_log'] = jnp.log(jax.random.uniform(next(ks), (DEPTH, N_HEADS_B), jnp.float32, 1.0, 16.0))
    inp['ssd_d'] = gain((DEPTH, N_HEADS_B))
    inp['ssd_norm'] = gain((DEPTH, D_SSD))
    inp['conf_dw_w'] = nrm((DEPTH, CONF_KERNEL, D_CONV), CONF_KERNEL ** -0.5)
    inp['conf_dw_b'] = nrm((DEPTH, D_CONV), 0.02)
    inp['conf_ln_g'] = gain((DEPTH, D_CONV))
    inp['conf_ln_b'] = nrm((DEPTH, D_CONV), 0.02)
    inp['ffn_w1'] = nrm((n_dense, D_MODEL, D_FF), d_in)
    inp['ffn_w3'] = nrm((n_dense, D_MODEL, D_FF), d_in)
    inp['ffn_w2'] = nrm((n_dense, D_FF, D_MODEL), D_FF ** -0.5)
    inp['moe_router'] = nrm((n_moe, D_MODEL, N_EXPERTS), d_in)
    inp['moe_w1'] = nrm((n_moe, N_EXPERTS, D_MODEL, D_FF_EXPERT), d_in)
    inp['moe_w3'] = nrm((n_moe, N_EXPERTS, D_MODEL, D_FF_EXPERT), d_in)
    inp['moe_w2'] = nrm((n_moe, N_EXPERTS, D_FF_EXPERT, D_MODEL), D_FF_EXPERT ** -0.5)
    inp['norm_final'] = gain((D_MODEL,))
    return inp


def reference(x_prompt, x_sample, c_prompt, c_sample, cache_k, cache_v, state_ssm, state_conv_ssd, state_conv_conf,
              w_ada, b_ada, norm_mix, norm_ffn, w_in, w_out, attn_lambda, attn_subln, ssd_conv_w, ssd_conv_b,
              ssd_dt_bias, ssd_a_log, ssd_d, ssd_norm, conf_dw_w, conf_dw_b, conf_ln_g, conf_ln_b,
              ffn_w1, ffn_w3, ffn_w2, moe_router, moe_w1, moe_w3, moe_w2, norm_final):
    bp, lp = x_prompt.shape[:2]
    ls = x_sample.shape[1]
    past_len = cache_k.shape[2]
    pos_p = jnp.arange(lp, dtype=jnp.int32)
    pos_s = past_len + jnp.arange(ls, dtype=jnp.int32)
    dtype = x_prompt.dtype
    fresh = (None, None,
             jnp.zeros((bp, N_HEADS_B, SSD_HEAD_DIM, SSD_STATE), dtype),
             jnp.zeros((bp, SSD_CONV - 1, CONV_DIM_B), dtype),
             jnp.zeros((bp, CONF_KERNEL - 1, D_CONV), dtype))
    xp, xs = x_prompt, x_sample
    kp, vp, sp, cbp, ccp = [], [], [], [], []
    ks_, vs_, ss_, cbs, ccs = [], [], [], [], []
    for l in range(DEPTH):
        p = {'w_ada': w_ada[l], 'b_ada': b_ada[l], 'norm_mix': norm_mix[l], 'norm_ffn': norm_ffn[l],
             'w_in': w_in[l], 'w_out': w_out[l], 'attn_lambda': attn_lambda[l], 'attn_subln': attn_subln[l],
             'ssd_conv_w': ssd_conv_w[l], 'ssd_conv_b': ssd_conv_b[l], 'ssd_dt_bias': ssd_dt_bias[l],
             'ssd_a_log': ssd_a_log[l], 'ssd_d': ssd_d[l], 'ssd_norm': ssd_norm[l],
             'conf_dw_w': conf_dw_w[l], 'conf_dw_b': conf_dw_b[l], 'conf_ln_g': conf_ln_g[l], 'conf_ln_b': conf_ln_b[l]}
        if l % 2 == 0:
            p['ffn_w1'], p['ffn_w3'], p['ffn_w2'] = ffn_w1[l // 2], ffn_w3[l // 2], ffn_w2[l // 2]
        else:
            p['moe_router'], p['moe_w1'] = moe_router[l // 2], moe_w1[l // 2]
            p['moe_w3'], p['moe_w2'] = moe_w3[l // 2], moe_w2[l // 2]
        xp, st = _layer(xp, c_prompt, pos_p, fresh, p, l)
        kp.append(st[0]); vp.append(st[1]); sp.append(st[2]); cbp.append(st[3]); ccp.append(st[4])
        past = (cache_k[l], cache_v[l], state_ssm[l], state_conv_ssd[l], state_conv_conf[l])
        xs, st = _layer(xs, c_sample, pos_s, past, p, l)
        ks_.append(st[0]); vs_.append(st[1]); ss_.append(st[2]); cbs.append(st[3]); ccs.append(st[4])
    y_prompt = _rmsnorm(xp, norm_final)
    y_sample = _rmsnorm(xs, norm_final)
    return (y_prompt, y_sample,
            jnp.stack(kp), jnp.stack(vp), jnp.stack(sp), jnp.stack(cbp), jnp.stack(ccp),
            jnp.stack(ks_), jnp.stack(vs_), jnp.stack(ss_), jnp.stack(cbs), jnp.stack(ccs))
```

```python
import functools
import math

import numpy as np
import jax
import jax.numpy as jnp
from jax import lax
from jax.experimental import pallas as pl
from jax.experimental.pallas import tpu as pltpu

F32 = jnp.float32
BF16 = jnp.bfloat16
U32 = jnp.uint32
I32 = jnp.int32

CHUNK = 64
ROPE_THETA = 10000.0
EPS = 1e-6

LANES = 128
SUBLANES = 8
VMEM_LIMIT_BYTES = 48 * 1024 * 1024

ROW_TILE = 512
ATTN_TILE = 512
SSD_CHUNK = 128
CONF_TILE = 256
EXPERT_TILE = 512
GATHER_TILE = 256


def _cparams(*sem):
    return pltpu.CompilerParams(dimension_semantics=sem, vmem_limit_bytes=VMEM_LIMIT_BYTES)


def _tile(n, pref):
    t = min(n, pref)
    assert n % t == 0, (n, pref)
    return t


def _rows_spec(arr, tm, width, by_col=False):
    if arr.shape[0] == 1:
        return pl.BlockSpec((1, width), lambda j, i: (0, j if by_col else 0))
    return pl.BlockSpec((tm, width), lambda j, i: (i, j if by_col else 0))


def _silu(x):
    return x * jax.nn.sigmoid(x)


def _dot(a, b):
    return jnp.dot(a, b, preferred_element_type=F32)


def _dot_nt(a, b):
    return lax.dot_general(a, b, (((1,), (1,)), ((), ())), preferred_element_type=F32)


def _adaln_kernel(c_ref, w_ref, b_ref, o_ref):
    s = _silu(c_ref[...]).astype(BF16)
    o_ref[0] = _dot(s, w_ref[0].astype(BF16)) + b_ref[0]


def _adaln(c_all, w_ada, b_ada):
    depth, d, mc = w_ada.shape
    rows = c_all.shape[0]
    tn = _tile(mc, 1024)
    return pl.pallas_call(
        _adaln_kernel,
        out_shape=jax.ShapeDtypeStruct((depth, rows, mc), F32),
        grid=(depth, mc // tn),
        in_specs=[pl.BlockSpec((rows, d), lambda l, j: (0, 0)),
                  pl.BlockSpec((1, d, tn), lambda l, j: (l, 0, j)),
                  pl.BlockSpec((1, 1, tn), lambda l, j: (l, 0, j))],
        out_specs=pl.BlockSpec((1, rows, tn), lambda l, j: (l, 0, j)),
        compiler_params=_cparams("parallel", "parallel"),
        name="adaln",
    )(c_all, w_ada, b_ada.reshape(depth, 1, mc))


def _modulated(x, g, sc, sh):
    ms = jnp.mean(x * x, axis=-1, keepdims=True)
    y = x * lax.rsqrt(ms + EPS) * g
    if sc is not None:
        y = y * (1.0 + sc) + sh
    return y


def _normmod_kernel(x_ref, g_ref, sc_ref, sh_ref, o_ref):
    o_ref[...] = _modulated(x_ref[...], g_ref[...], sc_ref[...], sh_ref[...]).astype(o_ref.dtype)


def _norm_kernel(x_ref, g_ref, o_ref):
    o_ref[...] = _modulated(x_ref[...], g_ref[...], None, None).astype(o_ref.dtype)


def _normmod(x, g, sc, sh, out_dtype):
    t, d = x.shape
    tm = _tile(t, ROW_TILE)
    g = g.reshape(1, d)
    row = pl.BlockSpec((tm, d), lambda j, i: (i, 0))
    vec = pl.BlockSpec((1, d), lambda j, i: (0, 0))
    if sc is None:
        body, ops, specs = _norm_kernel, (x, g), [row, vec]
    else:
        body, ops = _normmod_kernel, (x, g, sc, sh)
        specs = [row, vec, _rows_spec(sc, tm, d), _rows_spec(sh, tm, d)]
    return pl.pallas_call(
        body, out_shape=jax.ShapeDtypeStruct((t, d), out_dtype),
        grid=(1, t // tm), in_specs=specs, out_specs=row,
        compiler_params=_cparams("parallel", "parallel"), name="normmod",
    )(*ops)


def _proj_kernel(a_ref, w_ref, cos_ref, sin_ref, o_ref, *, n_rope_blocks):
    acc = _dot(a_ref[...], w_ref[...])
    j = pl.program_id(0)

    @pl.when(j >= n_rope_blocks)
    def _():
        o_ref[...] = acc

    @pl.when(j < n_rope_blocks)
    def _():
        cos = cos_ref[...]
        sin = sin_ref[...]
        lane = lax.broadcasted_iota(I32, cos.shape, 1)
        first_half = (lane % CHUNK) < (CHUNK // 2)
        for c in range(acc.shape[1] // LANES):
            blk = acc[:, c * LANES:(c + 1) * LANES]
            partner = jnp.where(first_half, pltpu.roll(blk, LANES - 32, 1), pltpu.roll(blk, 32, 1))
            o_ref[:, c * LANES:(c + 1) * LANES] = blk * cos + partner * sin


def _proj(h, w, cos, sin, tn, n_rope_blocks):
    t, d = h.shape
    n = w.shape[1]
    tm = _tile(t, ROW_TILE)
    return pl.pallas_call(
        functools.partial(_proj_kernel, n_rope_blocks=n_rope_blocks),
        out_shape=jax.ShapeDtypeStruct((t, n), F32),
        grid=(n // tn, t // tm),
        in_specs=[pl.BlockSpec((tm, d), lambda j, i: (i, 0)),
                  pl.BlockSpec((d, tn), lambda j, i: (0, j)),
                  pl.BlockSpec((tm, LANES), lambda j, i: (i, 0)),
                  pl.BlockSpec((tm, LANES), lambda j, i: (i, 0))],
        out_specs=pl.BlockSpec((tm, tn), lambda j, i: (i, j)),
        compiler_params=_cparams("parallel", "parallel"), name="proj",
    )(h, w, cos, sin)


def _lambda_value(lam_ref, lambda_init):
    lp = lam_ref[...]
    a = jnp.sum(lp[0:1] * lp[1:2], axis=-1, keepdims=True)
    b = jnp.sum(lp[2:3] * lp[3:4], axis=-1, keepdims=True)
    return jnp.exp(a) - jnp.exp(b) + lambda_init


def _stack_maps(q):
    lane = lax.broadcasted_iota(I32, q.shape, 1)
    lo = lane < (LANES // 2)
    return jnp.concatenate([jnp.where(lo, q, 0.0), jnp.where(lo, 0.0, q)], axis=0).astype(BF16)


def _diff_finish(o1, o2, lam, g, lambda_init):
    o = o1 - lam * o2
    ms = jnp.mean(o * o, axis=-1, keepdims=True)
    return o * lax.rsqrt(ms + EPS) * g * (1.0 - lambda_init)


def _flash_kernel(qi_ref, ki_ref, q_ref, k_ref, v_ref, lam_ref, g_ref, o_ref,
                  qs_ref, m_ref, l_ref, acc_ref, *, tq, lambda_init):
    p = pl.program_id(1)
    qi = qi_ref[p]
    ki = ki_ref[p]

    @pl.when(ki == 0)
    def _():
        qs_ref[...] = _stack_maps(q_ref[...] * (float(CHUNK) ** -0.5))
        m_ref[...] = jnp.full(m_ref.shape, -jnp.inf, F32)
        l_ref[...] = jnp.zeros(l_ref.shape, F32)
        acc_ref[...] = jnp.zeros(acc_ref.shape, F32)

    def step(diagonal):
        s = _dot_nt(qs_ref[...], k_ref[...].astype(BF16))
        if diagonal:
            row = lax.broadcasted_iota(I32, s.shape, 0) % tq
            col = lax.broadcasted_iota(I32, s.shape, 1)
            s = jnp.where((col // CHUNK) <= (row // CHUNK), s, -jnp.inf)
        m_prev = m_ref[...]
        m_new = jnp.maximum(m_prev, jnp.max(s, axis=1, keepdims=True))
        alpha = jnp.exp(m_prev - m_new)
        pr = jnp.exp(s - m_new)
        l_ref[...] = alpha * l_ref[...] + jnp.sum(pr, axis=1, keepdims=True)
        acc_ref[...] = alpha * acc_ref[...] + _dot(pr.astype(BF16), v_ref[...].astype(BF16))
        m_ref[...] = m_new

    @pl.when(ki < qi)
    def _():
        step(False)

    @pl.when(ki == qi)
    def _():
        step(True)
        o = acc_ref[...] / l_ref[...]
        lam = _lambda_value(lam_ref, lambda_init)
        o_ref[...] = _diff_finish(o[0:tq], o[tq:2 * tq], lam, g_ref[...], lambda_init).astype(o_ref.dtype)


def _flash_attention(proj, col_q, col_k, col_v, n_heads, lam_p, subln, lambda_init):
    t = proj.shape[0]
    tq = _tile(t, ATTN_TILE)
    assert tq % CHUNK == 0
    nq = t // tq
    pairs = [(a, b) for a in range(nq) for b in range(a + 1)]
    qi = jnp.asarray(np.array([a for a, _ in pairs], np.int32))
    ki = jnp.asarray(np.array([b for _, b in pairs], np.int32))
    cq, ck, cv = col_q // LANES, col_k // LANES, col_v // LANES
    grid_spec = pltpu.PrefetchScalarGridSpec(
        num_scalar_prefetch=2,
        grid=(n_heads, len(pairs)),
        in_specs=[pl.BlockSpec((tq, LANES), lambda h, p, qi, ki: (qi[p], cq + h)),
                  pl.BlockSpec((tq, LANES), lambda h, p, qi, ki: (ki[p], ck + h)),
                  pl.BlockSpec((tq, LANES), lambda h, p, qi, ki: (ki[p], cv + h)),
                  pl.BlockSpec(lam_p.shape, lambda h, p, qi, ki: (0, 0)),
                  pl.BlockSpec((1, LANES), lambda h, p, qi, ki: (0, 0))],
        out_specs=pl.BlockSpec((tq, LANES), lambda h, p, qi, ki: (qi[p], h)),
        scratch_shapes=[pltpu.VMEM((2 * tq, LANES), BF16),
                        pltpu.VMEM((2 * tq, 1), F32),
                        pltpu.VMEM((2 * tq, 1), F32),
                        pltpu.VMEM((2 * tq, LANES), F32)])
    return pl.pallas_call(
        functools.partial(_flash_kernel, tq=tq, lambda_init=lambda_init),
        out_shape=jax.ShapeDtypeStruct((t, n_heads * LANES), BF16),
        grid_spec=grid_spec,
        compiler_params=_cparams("parallel", "arbitrary"), name="flash_attn",
    )(qi, ki, proj, proj, proj, lam_p, subln.reshape(1, LANES))


def _cached_attn_kernel(q_ref, kn_ref, vn_ref, kc_ref, vc_ref, lam_ref, g_ref, o_ref, *, lambda_init):
    ln = q_ref.shape[0]
    past = kc_ref.shape[0]
    qs = _stack_maps(q_ref[...] * (float(CHUNK) ** -0.5))
    sc = _dot_nt(qs, kc_ref[...].astype(BF16))
    sn = _dot_nt(qs, kn_ref[...].astype(BF16))
    q_chunk_c = (past + lax.broadcasted_iota(I32, sc.shape, 0) % ln) // CHUNK
    q_chunk_n = (past + lax.broadcasted_iota(I32, sn.shape, 0) % ln) // CHUNK
    sc = jnp.where(lax.broadcasted_iota(I32, sc.shape, 1) // CHUNK <= q_chunk_c, sc, -jnp.inf)
    sn = jnp.where((past + lax.broadcasted_iota(I32, sn.shape, 1)) // CHUNK <= q_chunk_n, sn, -jnp.inf)
    m = jnp.maximum(jnp.max(sc, axis=1, keepdims=True), jnp.max(sn, axis=1, keepdims=True))
    pc = jnp.exp(sc - m)
    pn = jnp.exp(sn - m)
    denom = jnp.sum(pc, axis=1, keepdims=True) + jnp.sum(pn, axis=1, keepdims=True)
    o = (_dot(pc.astype(BF16), vc_ref[...].astype(BF16)) + _dot(pn.astype(BF16), vn_ref[...].astype(BF16))) / denom
    lam = _lambda_value(lam_ref, lambda_init)
    o_ref[...] = _diff_finish(o[0:ln], o[ln:2 * ln], lam, g_ref[...], lambda_init).astype(o_ref.dtype)


def _cached_attention(proj3, col_q, col_k, col_v, n_heads, k_cache, v_cache, lam_p, subln, lambda_init):
    b, ln, _ = proj3.shape
    past = k_cache.shape[1]
    cq, ck, cv = col_q // LANES, col_k // LANES, col_v // LANES
    new = lambda c: pl.BlockSpec((None, ln, LANES), lambda bi, h: (bi, 0, c + h))
    old = pl.BlockSpec((None, past, LANES), lambda bi, h: (bi, 0, h))
    return pl.pallas_call(
        functools.partial(_cached_attn_kernel, lambda_init=lambda_init),
        out_shape=jax.ShapeDtypeStruct((b, ln, n_heads * LANES), BF16),
        grid=(b, n_heads),
        in_specs=[new(cq), new(ck), new(cv), old, old,
                  pl.BlockSpec(lam_p.shape, lambda bi, h: (0, 0)),
                  pl.BlockSpec((1, LANES), lambda bi, h: (0, 0))],
        out_specs=pl.BlockSpec((None, ln, LANES), lambda bi, h: (bi, 0, h)),
        compiler_params=_cparams("parallel", "parallel"), name="cached_attn",
    )(proj3, proj3, proj3, k_cache, v_cache, lam_p, subln.reshape(1, LANES))


def _split3(x):
    x1 = x.astype(BF16)
    r1 = x - x1.astype(F32)
    x2 = r1.astype(BF16)
    x3 = (r1 - x2.astype(F32)).astype(BF16)
    return x1, x2, x3


def _ssd_kernel(xs_ref, bm_ref, cm_ref, hx_ref, hb_ref, hc_ref, sx_ref, sb_ref, sc_ref,
                wx_ref, wb_ref, wc_ref, bx_ref, bb_ref, bc_ref,
                dt_ref, dtb_ref, alog_ref, z_ref, dexp_ref, gn_ref, h0_ref,
                y_ref, hout_ref, nsx_ref, nsb_ref, nsc_ref,
                scr_x, scr_b, scr_c, h_scr, y_scr, xte_scr,
                *, q, qp, n_heads, p_dim, n_state, n_groups, width):
    i = pl.program_id(1)
    first = i == 0
    halo = SUBLANES

    @pl.when(first)
    def _():
        h_scr[...] = h0_ref[...]

    def conv(t_ref, halo_ref, st_ref, w_ref, b_ref, scr, ns_ref):
        scr[0:halo] = jnp.where(first, st_ref[...], halo_ref[...])
        scr[halo:halo + q] = t_ref[...]
        acc = b_ref[...] + w_ref[0:1, :] * scr[halo - width + 1:halo - width + 1 + q]
        for k in range(1, width):
            acc = acc + w_ref[k:k + 1, :] * scr[halo - width + 1 + k:halo - width + 1 + k + q]
        ns_ref[...] = scr[q:q + halo]
        return _silu(acc)

    xs = conv(xs_ref, hx_ref, sx_ref, wx_ref, bx_ref, scr_x, nsx_ref)
    bm = conv(bm_ref, hb_ref, sb_ref, wb_ref, bb_ref, scr_b, nsb_ref)
    cm = conv(cm_ref, hc_ref, sc_ref, wc_ref, bc_ref, scr_c, nsc_ref)
    dt = jax.nn.softplus(dt_ref[...] + dtb_ref[...])
    a = -jnp.exp(alog_ref[...])

    def pad(v):
        if qp == q:
            return v
        return jnp.concatenate([v, jnp.zeros((qp - q, v.shape[1]), v.dtype)], axis=0)

    xs_p, bm_p, cm_p, dt_p = pad(xs), pad(bm), pad(cm), pad(dt)
    d_a = dt_p * a
    row = lax.broadcasted_iota(I32, (qp, qp), 0)
    col = lax.broadcasted_iota(I32, (qp, qp), 1)
    tril = row >= col
    ones_tril = jnp.where(tril, 1.0, 0.0).astype(BF16)
    d1, d2, d3 = _split3(d_a)
    acum = _dot(ones_tril, d1) + _dot(ones_tril, d2) + _dot(ones_tril, d3)
    acum_t = acum.T
    total = acum[qp - 1:qp, :]
    bm_b = bm_p.astype(BF16)
    cm_b = cm_p.astype(BF16)
    scores = [_dot_nt(cm_b[:, g * n_state:(g + 1) * n_state], bm_b[:, g * n_state:(g + 1) * n_state])
              for g in range(n_groups)]
    rep = n_heads // n_groups
    for h in range(n_heads):
        g = h // rep
        colv = acum[:, h:h + 1]
        rowv = acum_t[h:h + 1, :]
        tot = total[:, h:h + 1]
        decay = jnp.where(tril, jnp.exp(colv - rowv), 0.0)
        xdt = xs_p[:, h * p_dim:(h + 1) * p_dim] * dt_p[:, h:h + 1]
        y_diag = _dot((scores[g] * decay).astype(BF16), xdt.astype(BF16))
        h_prev = h_scr[h]
        y_off = jnp.exp(colv) * _dot_nt(cm_b[:, g * n_state:(g + 1) * n_state], h_prev.astype(BF16))
        y_scr[:, h * p_dim:(h + 1) * p_dim] = y_diag + y_off
        xte_scr[:, h * p_dim:(h + 1) * p_dim] = xdt * jnp.exp(tot - colv)
    xte_t = xte_scr[...].T.astype(BF16)
    for h in range(n_heads):
        g = h // rep
        st = _dot(xte_t[h * p_dim:(h + 1) * p_dim, :], bm_b[:, g * n_state:(g + 1) * n_state])
        h_scr[h] = jnp.exp(total[:, h:h + 1]) * h_scr[h] + st
    y = y_scr[0:q] + dexp_ref[...] * xs
    yg = y * _silu(z_ref[...])
    ms = jnp.mean(yg * yg, axis=-1, keepdims=True)
    y_ref[...] = (yg * lax.rsqrt(ms + EPS) * gn_ref[...]).astype(y_ref.dtype)
    hout_ref[...] = h_scr[...]


def _ssd(proj3, cols, dims, conv_state, conv_w, conv_b, dt_bias, a_log, d_skip, norm_g, h0):
    b, ln, _ = proj3.shape
    n_heads, p_dim, n_state, n_groups, width = dims
    d_ssd = n_heads * p_dim
    gn = n_groups * n_state
    q = _tile(ln, SSD_CHUNK)
    qp = max(q, LANES)
    assert q % SUBLANES == 0 and ln >= width - 1 and width - 1 <= SUBLANES and n_heads <= LANES
    nsteps = ln // q
    halo = SUBLANES
    widths = (d_ssd, gn, gn)
    offs = (cols["xs"], cols["bm"], cols["cm"])
    for o, w in zip(offs, widths):
        assert o % w == 0
    assert cols["z"] % d_ssd == 0 and cols["dt"] % LANES == 0

    def tile_spec(o, w):
        return pl.BlockSpec((None, q, w), lambda bi, i: (bi, i, o // w))

    def halo_spec(o, w):
        return pl.BlockSpec((None, halo, w), lambda bi, i: (bi, jnp.maximum(i * (q // halo) - 1, 0), o // w))

    def state_spec(w):
        return pl.BlockSpec((None, halo, w), lambda bi, i: (bi, 0, 0))

    def const_spec(shape):
        return pl.BlockSpec(shape, lambda bi, i: (0,) * len(shape))

    st = jnp.pad(conv_state, ((0, 0), (halo - (width - 1), 0), (0, 0)))
    bounds = (0, d_ssd, d_ssd + gn, d_ssd + 2 * gn)
    st_segs = [st[..., bounds[k]:bounds[k + 1]] for k in range(3)]
    w_segs = [conv_w[:, bounds[k]:bounds[k + 1]] for k in range(3)]
    b_segs = [conv_b[bounds[k]:bounds[k + 1]].reshape(1, -1) for k in range(3)]
    pad_lane = lambda v: jnp.pad(v.reshape(1, -1), ((0, 0), (0, LANES - v.shape[0])))
    in_specs = ([tile_spec(o, w) for o, w in zip(offs, widths)]
                + [halo_spec(o, w) for o, w in zip(offs, widths)]
                + [state_spec(w) for w in widths]
                + [const_spec((width, w)) for w in widths]
                + [const_spec((1, w)) for w in widths]
                + [tile_spec(cols["dt"], LANES), const_spec((1, LANES)), const_spec((1, LANES)),
                   tile_spec(cols["z"], d_ssd), const_spec((1, d_ssd)), const_spec((1, d_ssd)),
                   pl.BlockSpec((None, n_heads, p_dim, n_state), lambda bi, i: (bi, 0, 0, 0))])
    out_shape = (jax.ShapeDtypeStruct((b, ln, d_ssd), BF16),
                 jax.ShapeDtypeStruct((b, n_heads, p_dim, n_state), F32),
                 jax.ShapeDtypeStruct((b, halo, d_ssd), F32),
                 jax.ShapeDtypeStruct((b, halo, gn), F32),
                 jax.ShapeDtypeStruct((b, halo, gn), F32))
    out_specs = (pl.BlockSpec((None, q, d_ssd), lambda bi, i: (bi, i, 0)),
                 pl.BlockSpec((None, n_heads, p_dim, n_state), lambda bi, i: (bi, 0, 0, 0)),
                 state_spec(d_ssd), state_spec(gn), state_spec(gn))
    scratch = [pltpu.VMEM((q + halo, d_ssd), F32), pltpu.VMEM((q + halo, gn), F32), pltpu.VMEM((q + halo, gn), F32),
               pltpu.VMEM((n_heads, p_dim, n_state), F32),
               pltpu.VMEM((qp, d_ssd), F32), pltpu.VMEM((qp, d_ssd), F32)]
    y, h_new, nsx, nsb, nsc = pl.pallas_call(
        functools.partial(_ssd_kernel, q=q, qp=qp, n_heads=n_heads, p_dim=p_dim, n_state=n_state,
                          n_groups=n_groups, width=width),
        out_shape=out_shape, grid=(b, nsteps), in_specs=in_specs, out_specs=out_specs,
        scratch_shapes=scratch,
        compiler_params=_cparams("parallel", "arbitrary"), name="ssd",
    )(proj3, proj3, proj3, proj3, proj3, proj3, *st_segs, *w_segs, *b_segs,
      proj3, pad_lane(dt_bias), pad_lane(a_log), proj3,
      jnp.repeat(d_skip, p_dim).reshape(1, d_ssd), norm_g.reshape(1, d_ssd), h0)
    new_state = jnp.concatenate([nsx, nsb, nsc], axis=-1)[:, halo - (width - 1):, :]
    return y, h_new, new_state


def _conf_kernel(ga_ref, gg_ref, hga_ref, hgg_ref, st_ref, w_ref, b_ref, lg_ref, lb_ref,
                 y_ref, ns_ref, scr, *, tb, width, halo):
    i = pl.program_id(1)
    u = ga_ref[...] * jax.nn.sigmoid(gg_ref[...])
    uh = hga_ref[...] * jax.nn.sigmoid(hgg_ref[...])
    scr[0:halo] = jnp.where(i == 0, st_ref[...], uh)
    scr[halo:halo + tb] = u
    base = halo - (width - 1)
    acc = b_ref[...] + w_ref[0:1, :] * scr[base:base + tb]
    for k in range(1, width):
        acc = acc + w_ref[k:k + 1, :] * scr[base + k:base + k + tb]
    mu = jnp.mean(acc, axis=-1, keepdims=True)
    cen = acc - mu
    var = jnp.mean(cen * cen, axis=-1, keepdims=True)
    y = cen * lax.rsqrt(var + EPS) * lg_ref[...] + lb_ref[...]
    y_ref[...] = _silu(y).astype(y_ref.dtype)
    ns_ref[...] = scr[tb:tb + halo]


def _conf(proj3, col_ga, col_gg, d_conv, conv_state, w, bias, ln_g, ln_b):
    b, ln, _ = proj3.shape
    width = w.shape[0]
    halo = -(-(width - 1) // SUBLANES) * SUBLANES
    tb = _tile(ln, CONF_TILE)
    assert tb % halo == 0 and ln >= width - 1 and col_ga % d_conv == 0 and col_gg % d_conv == 0
    st = jnp.pad(conv_state, ((0, 0), (halo - (width - 1), 0), (0, 0)))
    tile_spec = lambda o: pl.BlockSpec((None, tb, d_conv), lambda bi, i: (bi, i, o // d_conv))
    halo_spec = lambda o: pl.BlockSpec((None, halo, d_conv),
                                       lambda bi, i: (bi, jnp.maximum(i * (tb // halo) - 1, 0), o // d_conv))
    st_spec = pl.BlockSpec((None, halo, d_conv), lambda bi, i: (bi, 0, 0))
    vec = pl.BlockSpec((1, d_conv), lambda bi, i: (0, 0))
    y, ns = pl.pallas_call(
        functools.partial(_conf_kernel, tb=tb, width=width, halo=halo),
        out_shape=(jax.ShapeDtypeStruct((b, ln, d_conv), BF16), jax.ShapeDtypeStruct((b, halo, d_conv), F32)),
        grid=(b, ln // tb),
        in_specs=[tile_spec(col_ga), tile_spec(col_gg), halo_spec(col_ga), halo_spec(col_gg), st_spec,
                  pl.BlockSpec((width, d_conv), lambda bi, i: (0, 0)), vec, vec, vec],
        out_specs=(pl.BlockSpec((None, tb, d_conv), lambda bi, i: (bi, i, 0)), st_spec),
        scratch_shapes=[pltpu.VMEM((tb + halo, d_conv), F32)],
        compiler_params=_cparams("parallel", "arbitrary"), name="conf_conv",
    )(proj3, proj3, proj3, proj3, st, w, bias.reshape(1, -1), ln_g.reshape(1, -1), ln_b.reshape(1, -1))
    return y, ns[:, halo - (width - 1):, :]


def _outproj_kernel(ya_ref, yb_ref, yc_ref, wa_ref, wb_ref, wc_ref, res_ref, gate_ref, o_ref):
    acc = _dot(ya_ref[...], wa_ref[...]) + _dot(yb_ref[...], wb_ref[...]) + _dot(yc_ref[...], wc_ref[...])
    o_ref[...] = res_ref[...] + gate_ref[...] * acc


def _outproj(ya, yb, yc, w_out, res, gate):
    t, d = res.shape
    da, db = ya.shape[1], yb.shape[1]
    tm = _tile(t, ROW_TILE)
    tn = _tile(d, 1024)
    a_spec = lambda k: pl.BlockSpec((tm, k), lambda j, i: (i, 0))
    w_spec = lambda k: pl.BlockSpec((k, tn), lambda j, i: (0, j))
    return pl.pallas_call(
        _outproj_kernel, out_shape=jax.ShapeDtypeStruct((t, d), F32),
        grid=(d // tn, t // tm),
        in_specs=[a_spec(da), a_spec(db), a_spec(yc.shape[1]), w_spec(da), w_spec(db), w_spec(yc.shape[1]),
                  pl.BlockSpec((tm, tn), lambda j, i: (i, j)), _rows_spec(gate, tm, tn, by_col=True)],
        out_specs=pl.BlockSpec((tm, tn), lambda j, i: (i, j)),
        compiler_params=_cparams("parallel", "parallel"), name="outproj",
    )(ya, yb, yc, w_out[:da], w_out[da:da + db], w_out[da + db:], res, gate)


def _ffn_up_kernel(a_ref, w1_ref, w3_ref, o_ref):
    a = a_ref[...]
    o_ref[...] = (_silu(_dot(a, w1_ref[...])) * _dot(a, w3_ref[...])).astype(o_ref.dtype)


def _ffn_down_kernel(a_ref, w_ref, res_ref, gate_ref, o_ref):
    o_ref[...] = res_ref[...] + gate_ref[...] * _dot(a_ref[...], w_ref[...])


def _ffn(h, w1, w3, w2, res, gate):
    t, d = h.shape
    f = w1.shape[1]
    tm = _tile(t, ROW_TILE)
    tn = _tile(f, 512)
    up = pl.pallas_call(
        _ffn_up_kernel, out_shape=jax.ShapeDtypeStruct((t, f), BF16),
        grid=(f // tn, t // tm),
        in_specs=[pl.BlockSpec((tm, d), lambda j, i: (i, 0)),
                  pl.BlockSpec((d, tn), lambda j, i: (0, j)),
                  pl.BlockSpec((d, tn), lambda j, i: (0, j))],
        out_specs=pl.BlockSpec((tm, tn), lambda j, i: (i, j)),
        compiler_params=_cparams("parallel", "parallel"), name="ffn_up",
    )(h, w1, w3)
    tn2 = _tile(d, 512)
    return pl.pallas_call(
        _ffn_down_kernel, out_shape=jax.ShapeDtypeStruct((t, d), F32),
        grid=(d // tn2, t // tm),
        in_specs=[pl.BlockSpec((tm, f), lambda j, i: (i, 0)),
                  pl.BlockSpec((f, tn2), lambda j, i: (0, j)),
                  pl.BlockSpec((tm, tn2), lambda j, i: (i, j)),
                  _rows_spec(gate, tm, tn2, by_col=True)],
        out_specs=pl.BlockSpec((tm, tn2), lambda j, i: (i, j)),
        compiler_params=_cparams("parallel", "parallel"), name="ffn_down",
    )(up, w2, res, gate)


def _router_kernel(x_ref, g_ref, sc_ref, sh_ref, r_ref, hp_ref, sel_ref, gate_ref, tot_ref, *, n_experts):
    h = _modulated(x_ref[...], g_ref[...], sc_ref[...], sh_ref[...])
    half = h.shape[1] // 2
    lo = lax.bitcast_convert_type(h[:, :half].astype(BF16).astype(F32), U32)
    hi = lax.bitcast_convert_type(h[:, half:].astype(BF16).astype(F32), U32)
    hp_ref[...] = (lo >> 16) | hi
    r = r_ref[...]
    h1 = h.astype(BF16)
    h2 = (h - h1.astype(F32)).astype(BF16)
    r1 = r.astype(BF16)
    r2 = (r - r1.astype(F32)).astype(BF16)
    logits = _dot(h1, r1) + _dot(h1, r2) + _dot(h2, r1)
    lane = lax.broadcasted_iota(I32, logits.shape, 1)
    lane_f = lane.astype(F32)
    lg = jnp.where(lane < n_experts, logits, -jnp.inf)
    v1 = jnp.max(lg, axis=1, keepdims=True)
    i1 = jnp.min(jnp.where(lg == v1, lane_f, float(LANES)), axis=1, keepdims=True)
    lg2 = jnp.where(lane_f == i1, -jnp.inf, lg)
    v2 = jnp.max(lg2, axis=1, keepdims=True)
    i2 = jnp.min(jnp.where(lg2 == v2, lane_f, float(LANES)), axis=1, keepdims=True)
    e = jnp.exp(v2 - v1)
    g1 = 1.0 / (1.0 + e)
    g2 = e * g1
    first = lane_f == i1
    second = lane_f == i2
    sel_ref[...] = jnp.where(first, 1, jnp.where(second, 2, 0)).astype(I32)
    gate_ref[...] = jnp.where(lane == 0, g1, jnp.where(lane == 1, g2, 0.0))
    cnt = jnp.sum(jnp.where(first | second, 1.0, 0.0), axis=0, keepdims=True)

    @pl.when(pl.program_id(1) == 0)
    def _():
        tot_ref[...] = jnp.zeros(tot_ref.shape, F32)

    tot_ref[...] += cnt


def _router(x, g, sc, sh, router_w):
    t, d = x.shape
    n_experts = router_w.shape[1]
    tm = _tile(t, ROW_TILE)
    r = jnp.pad(router_w, ((0, 0), (0, LANES - n_experts)))
    row = lambda w: pl.BlockSpec((tm, w), lambda j, i: (i, 0))
    return pl.pallas_call(
        functools.partial(_router_kernel, n_experts=n_experts),
        out_shape=(jax.ShapeDtypeStruct((t, d // 2), U32), jax.ShapeDtypeStruct((t, LANES), I32),
                   jax.ShapeDtypeStruct((t, LANES), F32), jax.ShapeDtypeStruct((1, LANES), F32)),
        grid=(1, t // tm),
        in_specs=[row(d), pl.BlockSpec((1, d), lambda j, i: (0, 0)), _rows_spec(sc, tm, d), _rows_spec(sh, tm, d),
                  pl.BlockSpec((d, LANES), lambda j, i: (0, 0))],
        out_specs=(row(d // 2), row(LANES), row(LANES), pl.BlockSpec((1, LANES), lambda j, i: (0, 0))),
        compiler_params=_cparams("arbitrary", "arbitrary"), name="router",
    )(x, g.reshape(1, d), sc, sh, r)


def _route_kernel(sel_ref, off_ref, pos_ref, carry_ref):
    @pl.when(pl.program_id(0) == 0)
    def _():
        carry_ref[...] = jnp.zeros(carry_ref.shape, F32)

    s = sel_ref[...]
    tm = s.shape[0]
    cnt = jnp.where(s > 0, 1.0, 0.0)
    row = lax.broadcasted_iota(I32, (tm, tm), 0)
    col = lax.broadcasted_iota(I32, (tm, tm), 1)
    before = jnp.where(row > col, 1.0, 0.0).astype(BF16)
    dest = _dot(before, cnt.astype(BF16)) + carry_ref[...] + off_ref[...]
    p1 = jnp.sum(jnp.where(s == 1, dest, 0.0), axis=1, keepdims=True)
    p2 = jnp.sum(jnp.where(s == 2, dest, 0.0), axis=1, keepdims=True)
    lane = lax.broadcasted_iota(I32, s.shape, 1)
    pos_ref[...] = jnp.where(lane == 0, p1, jnp.where(lane == 1, p2, 0.0)).astype(I32)
    carry_ref[...] += jnp.sum(cnt, axis=0, keepdims=True)


def _route(sel, offsets):
    t = sel.shape[0]
    tm = _tile(t, ROW_TILE)
    return pl.pallas_call(
        _route_kernel, out_shape=jax.ShapeDtypeStruct((t, LANES), I32),
        grid=(t // tm,),
        in_specs=[pl.BlockSpec((tm, LANES), lambda i: (i, 0)), pl.BlockSpec((1, LANES), lambda i: (0, 0))],
        out_specs=pl.BlockSpec((tm, LANES), lambda i: (i, 0)),
        scratch_shapes=[pltpu.VMEM((1, LANES), F32)],
        compiler_params=_cparams("arbitrary"), name="route",
    )(sel, offsets)


def _dispatch_kernel(p1_ref, p2_ref, hp_ref, init_ref, xs_ref, sem, *, tr):
    del init_ref
    base = pl.program_id(0) * tr

    def copy(r, dst):
        return pltpu.make_async_copy(hp_ref.at[pl.ds(r, 1), :], xs_ref.at[pl.ds(dst, 1), :], sem)

    def issue(r, c):
        copy(r, p1_ref[base + r]).start()
        copy(r, p2_ref[base + r]).start()
        return c

    def drain(r, c):
        copy(r, p1_ref[base + r]).wait()
        copy(r, p2_ref[base + r]).wait()
        return c

    lax.fori_loop(0, tr, issue, 0)
    lax.fori_loop(0, tr, drain, 0)


def _dispatch(hp, pos1, pos2, n_rows):
    t, w = hp.shape
    tr = _tile(t, GATHER_TILE)
    grid_spec = pltpu.PrefetchScalarGridSpec(
        num_scalar_prefetch=2, grid=(t // tr,),
        in_specs=[pl.BlockSpec((tr, w), lambda i, p1, p2: (i, 0)), pl.BlockSpec(memory_space=pl.ANY)],
        out_specs=pl.BlockSpec(memory_space=pl.ANY),
        scratch_shapes=[pltpu.SemaphoreType.DMA(())])
    return pl.pallas_call(
        functools.partial(_dispatch_kernel, tr=tr),
        out_shape=jax.ShapeDtypeStruct((n_rows, w), U32), grid_spec=grid_spec,
        input_output_aliases={3: 0},
        compiler_params=_cparams("arbitrary"), name="dispatch",
    )(pos1, pos2, hp, jnp.zeros((n_rows, w), U32))


def _unpack(u):
    lo = lax.bitcast_convert_type(u << 16, F32).astype(BF16)
    hi = lax.bitcast_convert_type(u & jnp.uint32(0xFFFF0000), F32).astype(BF16)
    return lo, hi


def _moe_up_kernel(te_ref, nv_ref, xs_ref, w1_ref, w3_ref, o_ref):
    i = pl.program_id(1)

    @pl.when(i < nv_ref[0])
    def _():
        lo, hi = _unpack(xs_ref[...])
        half = lo.shape[1]
        a = _dot(lo, w1_ref[0:half, :]) + _dot(hi, w1_ref[half:2 * half, :])
        b = _dot(lo, w3_ref[0:half, :]) + _dot(hi, w3_ref[half:2 * half, :])
        o_ref[...] = (_silu(a) * b).astype(o_ref.dtype)

    @pl.when(i >= nv_ref[0])
    def _():
        o_ref[...] = jnp.zeros(o_ref.shape, o_ref.dtype)


def _moe_down_kernel(te_ref, nv_ref, a_ref, w_ref, o_ref):
    i = pl.program_id(1)

    @pl.when(i < nv_ref[0])
    def _():
        o_ref[...] = _dot(a_ref[...], w_ref[...])

    @pl.when(i >= nv_ref[0])
    def _():
        o_ref[...] = jnp.zeros(o_ref.shape, o_ref.dtype)


def _moe_experts(xs, tile_expert, n_valid, w1, w3, w2, tm):
    n_rows, half = xs.shape
    _, d, f = w1.shape
    n_tiles = n_rows // tm
    tn = _tile(f, 512)
    live = lambda i, nv: jnp.minimum(i, nv[0] - 1)
    up = pl.pallas_call(
        _moe_up_kernel, out_shape=jax.ShapeDtypeStruct((n_rows, f), BF16),
        grid_spec=pltpu.PrefetchScalarGridSpec(
            num_scalar_prefetch=2, grid=(f // tn, n_tiles),
            in_specs=[pl.BlockSpec((tm, half), lambda j, i, te, nv: (live(i, nv), 0)),
                      pl.BlockSpec((None, d, tn), lambda j, i, te, nv: (te[live(i, nv)], 0, j)),
                      pl.BlockSpec((None, d, tn), lambda j, i, te, nv: (te[live(i, nv)], 0, j))],
            out_specs=pl.BlockSpec((tm, tn), lambda j, i, te, nv: (i, j))),
        compiler_params=_cparams("parallel", "arbitrary"), name="moe_up",
    )(tile_expert, n_valid, xs, w1, w3)
    tn2 = _tile(d, 512)
    return pl.pallas_call(
        _moe_down_kernel, out_shape=jax.ShapeDtypeStruct((n_rows, d), F32),
        grid_spec=pltpu.PrefetchScalarGridSpec(
            num_scalar_prefetch=2, grid=(d // tn2, n_tiles),
            in_specs=[pl.BlockSpec((tm, f), lambda j, i, te, nv: (live(i, nv), 0)),
                      pl.BlockSpec((None, f, tn2), lambda j, i, te, nv: (te[live(i, nv)], 0, j))],
            out_specs=pl.BlockSpec((tm, tn2), lambda j, i, te, nv: (i, j))),
        compiler_params=_cparams("parallel", "arbitrary"), name="moe_down",
    )(tile_expert, n_valid, up, w2)


def _combine_kernel(p1_ref, p2_ref, x_ref, gates_ref, gmod_ref, y_ref, o_ref, buf1, buf2, sem, *, tr):
    base = pl.program_id(0) * tr

    def copy(src, buf, r):
        return pltpu.make_async_copy(y_ref.at[pl.ds(src, 1), :], buf.at[pl.ds(r, 1), :], sem)

    def issue(r, c):
        copy(p1_ref[base + r], buf1, r).start()
        copy(p2_ref[base + r], buf2, r).start()
        return c

    def drain(r, c):
        copy(p1_ref[base + r], buf1, r).wait()
        copy(p2_ref[base + r], buf2, r).wait()
        return c

    lax.fori_loop(0, tr, issue, 0)
    lax.fori_loop(0, tr, drain, 0)
    g = gates_ref[...]
    f = g[:, 0:1] * buf1[...] + g[:, 1:2] * buf2[...]
    o_ref[...] = x_ref[...] + gmod_ref[...] * f


def _combine(x, gates, gmod, y, pos1, pos2):
    t, d = x.shape
    tr = _tile(t, GATHER_TILE)
    gspec = (pl.BlockSpec((1, d), lambda i, p1, p2: (0, 0)) if gmod.shape[0] == 1
             else pl.BlockSpec((tr, d), lambda i, p1, p2: (i, 0)))
    grid_spec = pltpu.PrefetchScalarGridSpec(
        num_scalar_prefetch=2, grid=(t // tr,),
        in_specs=[pl.BlockSpec((tr, d), lambda i, p1, p2: (i, 0)),
                  pl.BlockSpec((tr, LANES), lambda i, p1, p2: (i, 0)),
                  gspec, pl.BlockSpec(memory_space=pl.ANY)],
        out_specs=pl.BlockSpec((tr, d), lambda i, p1, p2: (i, 0)),
        scratch_shapes=[pltpu.VMEM((tr, d), F32), pltpu.VMEM((tr, d), F32), pltpu.SemaphoreType.DMA(())])
    return pl.pallas_call(
        functools.partial(_combine_kernel, tr=tr),
        out_shape=jax.ShapeDtypeStruct((t, d), F32), grid_spec=grid_spec,
        compiler_params=_cparams("arbitrary"), name="combine",
    )(pos1, pos2, x, gates, gmod, y)


def _moe(x, g, sc, sh, gmod, router_w, w1, w3, w2):
    t, d = x.shape
    n_experts = router_w.shape[1]
    tm = min(EXPERT_TILE, max(SUBLANES * 2, (2 * t) // n_experts))
    n_tiles = -(-2 * t // tm) + n_experts
    hp, sel, gates, totals = _router(x, g, sc, sh, router_w)
    counts = totals[0, :n_experts].astype(I32)
    tiles_per = (counts + tm - 1) // tm
    tile_end = jnp.cumsum(tiles_per)
    offsets = jnp.pad(((tile_end - tiles_per) * tm).astype(F32), (0, LANES - n_experts)).reshape(1, LANES)
    tile_expert = jnp.minimum(jnp.sum(jnp.arange(n_tiles, dtype=I32)[:, None] >= tile_end[None, :], axis=1),
                              n_experts - 1).astype(I32)
    n_valid = tile_end[n_experts - 1:].astype(I32)
    pos = _route(sel, offsets)
    pos1, pos2 = pos[:, 0], pos[:, 1]
    xs = _dispatch(hp, pos1, pos2, n_tiles * tm)
    y = _moe_experts(xs, tile_expert, n_valid, w1, w3, w2, tm)
    return _combine(x, gates, gmod, y, pos1, pos2)


def _rope_tables(pos):
    half = CHUNK // 2
    inv = ROPE_THETA ** (-jnp.arange(half, dtype=F32) / half)
    ang = pos.astype(F32)[:, None] * inv[None, :]
    cos, sin = jnp.cos(ang), jnp.sin(ang)
    reps = LANES // CHUNK
    return jnp.tile(cos, (1, 2 * reps)), jnp.tile(jnp.concatenate([-sin, sin], axis=1), (1, reps))


def _lambda_init(layer):
    return 0.8 - 0.6 * math.exp(-0.3 * layer)


def _layout(d_attn, d_ssd, gn, n_heads_b, d_conv):
    src = {"q": 0, "k": d_attn, "v": 2 * d_attn, "z": 3 * d_attn, "xs": 3 * d_attn + d_ssd,
           "bm": 3 * d_attn + 2 * d_ssd, "cm": 3 * d_attn + 2 * d_ssd + gn,
           "dt": 3 * d_attn + 2 * d_ssd + 2 * gn, "ga": 3 * d_attn + 2 * d_ssd + 2 * gn + n_heads_b,
           "gg": 3 * d_attn + 2 * d_ssd + 2 * gn + n_heads_b + d_conv}
    width = {"q": d_attn, "k": d_attn, "v": d_attn, "z": d_ssd, "xs": d_ssd, "bm": gn, "cm": gn,
             "dt": n_heads_b, "ga": d_conv, "gg": d_conv}
    order = ["q", "k", "v", "xs", "bm", "cm", "z", "ga", "gg", "dt"]
    cols, off = {}, 0
    for name in order:
        cols[name] = off
        off += LANES if name == "dt" else width[name]
    tn = min(1024, d_attn)
    total = -(-off // tn) * tn
    return src, width, order, cols, total, tn


def _permute_w_in(w, src, width, order, total):
    parts = []
    for name in order:
        blk = w[:, src[name]:src[name] + width[name]]
        if name == "dt":
            blk = jnp.pad(blk, ((0, 0), (0, LANES - width[name])))
        parts.append(blk)
    out = jnp.concatenate(parts, axis=1)
    return jnp.pad(out, ((0, 0), (0, total - out.shape[1]))).astype(BF16)


def kernel(x_prompt, x_sample, c_prompt, c_sample, cache_k, cache_v, state_ssm, state_conv_ssd, state_conv_conf,
           w_ada, b_ada, norm_mix, norm_ffn, w_in, w_out, attn_lambda, attn_subln, ssd_conv_w, ssd_conv_b,
           ssd_dt_bias, ssd_a_log, ssd_d, ssd_norm, conf_dw_w, conf_dw_b, conf_ln_g, conf_ln_b,
           ffn_w1, ffn_w3, ffn_w2, moe_router, moe_w1, moe_w3, moe_w2, norm_final):
    bp, lp, d = x_prompt.shape
    bs, ls, _ = x_sample.shape
    depth = w_in.shape[0]
    past = cache_k.shape[2]
    n_heads_a, head_dim = cache_k.shape[3], cache_k.shape[5]
    assert 2 * head_dim == LANES and head_dim == CHUNK and bp == 1
    d_attn = n_heads_a * 2 * head_dim
    n_heads_b, p_dim, n_state = state_ssm.shape[2:]
    d_ssd = n_heads_b * p_dim
    conv_dim_b = state_conv_ssd.shape[3]
    gn = (conv_dim_b - d_ssd) // 2
    n_groups = gn // n_state
    d_conv = state_conv_conf.shape[3]
    ssd_dims = (n_heads_b, p_dim, n_state, n_groups, ssd_conv_w.shape[1])
    src, width, order, cols, n_total, tn_proj = _layout(d_attn, d_ssd, gn, n_heads_b, d_conv)
    n_rope_blocks = 2 * d_attn // tn_proj

    tp, ts = bp * lp, bs * ls
    pos_p = jnp.arange(lp, dtype=I32)
    pos_s = past + jnp.arange(ls, dtype=I32)
    cos_p, sin_p = _rope_tables(pos_p)
    cos_s, sin_s = _rope_tables(jnp.tile(pos_s, bs))

    c_all = jnp.concatenate([c_prompt, c_sample], axis=0)
    c_rows = -(-c_all.shape[0] // (2 * SUBLANES)) * (2 * SUBLANES)
    mods = _adaln(jnp.pad(c_all, ((0, c_rows - c_all.shape[0]), (0, 0))), w_ada, b_ada)

    xp = x_prompt.reshape(tp, d)
    xs = x_sample.reshape(ts, d)
    zeros_ssm = jnp.zeros((bp, n_heads_b, p_dim, n_state), F32)
    zeros_cb = jnp.zeros((bp, ssd_conv_w.shape[1] - 1, conv_dim_b), F32)
    zeros_cc = jnp.zeros((bp, conf_dw_w.shape[1] - 1, d_conv), F32)
    outs_p, outs_s = [], []

    for l in range(depth):
        lam0 = _lambda_init(l)
        w_in_l = _permute_w_in(w_in[l], src, width, order, n_total)
        w_out_l = w_out[l].astype(BF16)
        mod_p = [mods[l, 0:bp, j * d:(j + 1) * d] for j in range(6)]
        mod_s = [jnp.repeat(mods[l, bp:bp + bs, j * d:(j + 1) * d], ls, axis=0) for j in range(6)]
        if l % 2 == 0:
            ffn_w = (ffn_w1[l // 2].astype(BF16), ffn_w3[l // 2].astype(BF16), ffn_w2[l // 2].astype(BF16))
        else:
            ffn_w = (moe_router[l // 2], moe_w1[l // 2].astype(BF16), moe_w3[l // 2].astype(BF16),
                     moe_w2[l // 2].astype(BF16))

        def run(x, nb, ln, mod, cos, sin, caches, ssm0, conv_b0, conv_c0):
            sh1, sc1, g1, sh2, sc2, g2 = mod
            h = _normmod(x, norm_mix[l], sc1, sh1, BF16)
            proj = _proj(h, w_in_l, cos, sin, tn_proj, n_rope_blocks)
            proj3 = proj.reshape(nb, ln, n_total)
            if caches is None:
                ya = _flash_attention(proj, cols["q"], cols["k"], cols["v"], n_heads_a,
                                      attn_lambda[l], attn_subln[l], lam0)
            else:
                ya = _cached_attention(proj3, cols["q"], cols["k"], cols["v"], n_heads_a, caches[0], caches[1],
                                       attn_lambda[l], attn_subln[l], lam0).reshape(nb * ln, d_attn)
            yb, ssm_new, conv_b_new = _ssd(proj3, cols, ssd_dims, conv_b0, ssd_conv_w[l], ssd_conv_b[l],
                                           ssd_dt_bias[l], ssd_a_log[l], ssd_d[l], ssd_norm[l], ssm0)
            yc, conv_c_new = _conf(proj3, cols["ga"], cols["gg"], d_conv, conv_c0, conf_dw_w[l], conf_dw_b[l],
                                   conf_ln_g[l], conf_ln_b[l])
            x = _outproj(ya, yb.reshape(nb * ln, d_ssd), yc.reshape(nb * ln, d_conv), w_out_l, x, g1)
            if l % 2 == 0:
                h2 = _normmod(x, norm_ffn[l], sc2, sh2, BF16)
                x = _ffn(h2, *ffn_w, x, g2)
            else:
                x = _moe(x, norm_ffn[l], sc2, sh2, g2, *ffn_w)
            k_new = proj3[:, :, cols["k"]:cols["k"] + d_attn].reshape(nb, ln, n_heads_a, 2, head_dim)
            v_new = proj3[:, :, cols["v"]:cols["v"] + d_attn].reshape(nb, ln, n_heads_a, 2 * head_dim)
            return x, (k_new, v_new, ssm_new, conv_b_new, conv_c_new)

        xp, st_p = run(xp, bp, lp, mod_p, cos_p, sin_p, None, zeros_ssm, zeros_cb, zeros_cc)
        caches = (cache_k[l].reshape(bs, past, d_attn), cache_v[l].reshape(bs, past, d_attn))
        xs, st_s = run(xs, bs, ls, mod_s, cos_s, sin_s, caches, state_ssm[l], state_conv_ssd[l], state_conv_conf[l])
        outs_p.append(st_p)
        outs_s.append(st_s)

    y_prompt = _normmod(xp, norm_final, None, None, F32).reshape(bp, lp, d)
    y_sample = _normmod(xs, norm_final, None, None, F32).reshape(bs, ls, d)
    stack = lambda outs, k: jnp.stack([o[k] for o in outs])
    return (y_prompt, y_sample,
            stack(outs_p, 0), stack(outs_p, 1), stack(outs_p, 2), stack(outs_p, 3), stack(outs_p, 4),
            stack(outs_s, 0), stack(outs_s, 1), stack(outs_s, 2), stack(outs_s, 3), stack(outs_s, 4))
```

```python
import functools
import math

import numpy as np
import jax
import jax.numpy as jnp
from jax import lax
from jax.experimental import pallas as pl
from jax.experimental.pallas import tpu as pltpu

F32 = jnp.float32
BF16 = jnp.bfloat16
U32 = jnp.uint32
I32 = jnp.int32

CHUNK = 64
ROPE_THETA = 10000.0
EPS = 1e-6

LANES = 128
SUBLANES = 8
VMEM_LIMIT_BYTES = 48 * 1024 * 1024

ROW_TILE = 512
ATTN_TQ = 1024
ATTN_TK = 512
ATTN_COLS = 2048
LOG2_E = 1.4426950408889634
SSD_CHUNK = 128
CONF_TILE = 256
EXPERT_TILE = 512
GATHER_TILE = 256


def _cparams(*sem):
    return pltpu.CompilerParams(dimension_semantics=sem, vmem_limit_bytes=VMEM_LIMIT_BYTES)


def _tile(n, pref):
    t = min(n, pref)
    assert n % t == 0, (n, pref)
    return t


def _rows_spec(arr, tm, width, by_col=False):
    if arr.shape[0] == 1:
        return pl.BlockSpec((1, width), lambda j, i: (0, j if by_col else 0))
    return pl.BlockSpec((tm, width), lambda j, i: (i, j if by_col else 0))


def _silu(x):
    return x * jax.nn.sigmoid(x)


def _dot(a, b):
    return jnp.dot(a, b, preferred_element_type=F32)


def _dot_nt(a, b):
    return lax.dot_general(a, b, (((1,), (1,)), ((), ())), preferred_element_type=F32)


def _adaln_kernel(c_ref, w_ref, b_ref, o_ref):
    s = _silu(c_ref[...]).astype(BF16)
    o_ref[0] = _dot(s, w_ref[0].astype(BF16)) + b_ref[0]


def _adaln(c_all, w_ada, b_ada):
    depth, d, mc = w_ada.shape
    rows = c_all.shape[0]
    tn = _tile(mc, 1024)
    return pl.pallas_call(
        _adaln_kernel,
        out_shape=jax.ShapeDtypeStruct((depth, rows, mc), F32),
        grid=(depth, mc // tn),
        in_specs=[pl.BlockSpec((rows, d), lambda l, j: (0, 0)),
                  pl.BlockSpec((1, d, tn), lambda l, j: (l, 0, j)),
                  pl.BlockSpec((1, 1, tn), lambda l, j: (l, 0, j))],
        out_specs=pl.BlockSpec((1, rows, tn), lambda l, j: (l, 0, j)),
        compiler_params=_cparams("parallel", "parallel"),
        name="adaln",
    )(c_all, w_ada, b_ada.reshape(depth, 1, mc))


def _modulated(x, g, sc, sh):
    ms = jnp.mean(x * x, axis=-1, keepdims=True)
    y = x * lax.rsqrt(ms + EPS) * g
    if sc is not None:
        y = y * (1.0 + sc) + sh
    return y


def _normmod_kernel(x_ref, g_ref, sc_ref, sh_ref, o_ref):
    o_ref[...] = _modulated(x_ref[...], g_ref[...], sc_ref[...], sh_ref[...]).astype(o_ref.dtype)


def _norm_kernel(x_ref, g_ref, o_ref):
    o_ref[...] = _modulated(x_ref[...], g_ref[...], None, None).astype(o_ref.dtype)


def _normmod(x, g, sc, sh, out_dtype):
    t, d = x.shape
    tm = _tile(t, ROW_TILE)
    g = g.reshape(1, d)
    row = pl.BlockSpec((tm, d), lambda j, i: (i, 0))
    vec = pl.BlockSpec((1, d), lambda j, i: (0, 0))
    if sc is None:
        body, ops, specs = _norm_kernel, (x, g), [row, vec]
    else:
        body, ops = _normmod_kernel, (x, g, sc, sh)
        specs = [row, vec, _rows_spec(sc, tm, d), _rows_spec(sh, tm, d)]
    return pl.pallas_call(
        body, out_shape=jax.ShapeDtypeStruct((t, d), out_dtype),
        grid=(1, t // tm), in_specs=specs, out_specs=row,
        compiler_params=_cparams("parallel", "parallel"), name="normmod",
    )(*ops)


def _proj_kernel(a_ref, w_ref, cos_ref, sin_ref, o_ref, *, n_rope_blocks):
    acc = _dot(a_ref[...], w_ref[...])
    j = pl.program_id(0)

    @pl.when(j >= n_rope_blocks)
    def _():
        o_ref[...] = acc

    @pl.when(j < n_rope_blocks)
    def _():
        cos = cos_ref[...]
        sin = sin_ref[...]
        lane = lax.broadcasted_iota(I32, cos.shape, 1)
        first_half = (lane % CHUNK) < (CHUNK // 2)
        for c in range(acc.shape[1] // LANES):
            blk = acc[:, c * LANES:(c + 1) * LANES]
            partner = jnp.where(first_half, pltpu.roll(blk, LANES - 32, 1), pltpu.roll(blk, 32, 1))
            o_ref[:, c * LANES:(c + 1) * LANES] = blk * cos + partner * sin


def _proj(h, w, cos, sin, tn, n_rope_blocks):
    t, d = h.shape
    n = w.shape[1]
    tm = _tile(t, ROW_TILE)
    return pl.pallas_call(
        functools.partial(_proj_kernel, n_rope_blocks=n_rope_blocks),
        out_shape=jax.ShapeDtypeStruct((t, n), F32),
        grid=(n // tn, t // tm),
        in_specs=[pl.BlockSpec((tm, d), lambda j, i: (i, 0)),
                  pl.BlockSpec((d, tn), lambda j, i: (0, j)),
                  pl.BlockSpec((tm, LANES), lambda j, i: (i, 0)),
                  pl.BlockSpec((tm, LANES), lambda j, i: (i, 0))],
        out_specs=pl.BlockSpec((tm, tn), lambda j, i: (i, j)),
        compiler_params=_cparams("parallel", "parallel"), name="proj",
    )(h, w, cos, sin)


def _lambda_value(lam_ref, lambda_init):
    lp = lam_ref[...]
    a = jnp.sum(lp[0:1] * lp[1:2], axis=-1, keepdims=True)
    b = jnp.sum(lp[2:3] * lp[3:4], axis=-1, keepdims=True)
    return jnp.exp(a) - jnp.exp(b) + lambda_init


def _stack_maps(q):
    lane = lax.broadcasted_iota(I32, q.shape, 1)
    lo = lane < (LANES // 2)
    return jnp.concatenate([jnp.where(lo, q, 0.0), jnp.where(lo, 0.0, q)], axis=0).astype(BF16)


def _diff_finish(o1, o2, lam, g, lambda_init):
    o = o1 - lam * o2
    ms = jnp.mean(o * o, axis=-1, keepdims=True)
    return o * lax.rsqrt(ms + EPS) * g * (1.0 - lambda_init)


def _flash_kernel(qi_ref, ki_ref, q_ref, k_ref, vt_ref, lam_ref, g_ref, o_ref,
                  qst_ref, m_ref, l_ref, acc_ref, *, tq, tk, lambda_init):
    p = pl.program_id(1)
    qi = qi_ref[p]
    ki = ki_ref[p]
    cw = min(ATTN_COLS, 2 * tq)

    @pl.when(ki == 0)
    def _():
        qt = (q_ref[...] * (float(CHUNK) ** -0.5 * LOG2_E)).T
        lo = lax.broadcasted_iota(I32, qt.shape, 0) < (LANES // 2)
        qst_ref[:, 0:tq] = jnp.where(lo, qt, 0.0).astype(BF16)
        qst_ref[:, tq:2 * tq] = jnp.where(lo, 0.0, qt).astype(BF16)
        m_ref[...] = jnp.full(m_ref.shape, -jnp.inf, F32)
        l_ref[...] = jnp.zeros(l_ref.shape, F32)
        acc_ref[...] = jnp.zeros(acc_ref.shape, F32)

    def step(masked):
        for c in range(2 * tq // cw):
            cs = slice(c * cw, (c + 1) * cw)
            st = _dot(k_ref[...], qst_ref[:, cs])
            if masked:
                k_chunk = (ki * tk + lax.broadcasted_iota(I32, st.shape, 0)) // CHUNK
                q_chunk = (qi * tq + (c * cw + lax.broadcasted_iota(I32, st.shape, 1)) % tq) // CHUNK
                st = jnp.where(k_chunk <= q_chunk, st, -jnp.inf)
            m_prev = m_ref[:, cs]
            m_new = jnp.maximum(m_prev, jnp.max(st, axis=0, keepdims=True))
            alpha = jnp.exp2(m_prev - m_new)
            pr = jnp.exp2(st - m_new)
            l_ref[:, cs] = alpha * l_ref[:, cs] + jnp.sum(pr, axis=0, keepdims=True)
            acc_ref[:, cs] = alpha * acc_ref[:, cs] + _dot(vt_ref[...], pr.astype(BF16))
            m_ref[:, cs] = m_new

    needs_mask = (ki + 1) * tk > qi * tq

    @pl.when(jnp.logical_not(needs_mask))
    def _():
        step(False)

    @pl.when(needs_mask)
    def _():
        step(True)

    @pl.when((ki + 1) * tk == (qi + 1) * tq)
    def _():
        o = acc_ref[...] * (1.0 / l_ref[...])
        lam = _lambda_value(lam_ref, lambda_init)
        o = (o[:, 0:tq] - lam * o[:, tq:2 * tq]).T
        ms = jnp.mean(o * o, axis=-1, keepdims=True)
        o_ref[...] = (o * lax.rsqrt(ms + EPS) * g_ref[...] * (1.0 - lambda_init)).astype(o_ref.dtype)


def _flash_attention(proj, k_b, v_t, col_q, n_heads, lam_p, subln, lambda_init):
    t = proj.shape[0]
    tq = _tile(t, ATTN_TQ)
    tk = _tile(tq, ATTN_TK)
    assert tk % CHUNK == 0
    per_q = tq // tk
    pairs = [(a, b) for a in range(t // tq) for b in range((a + 1) * per_q)]
    qi = jnp.asarray(np.array([a for a, _ in pairs], np.int32))
    ki = jnp.asarray(np.array([b for _, b in pairs], np.int32))
    cq = col_q // LANES
    grid_spec = pltpu.PrefetchScalarGridSpec(
        num_scalar_prefetch=2,
        grid=(n_heads, len(pairs)),
        in_specs=[pl.BlockSpec((tq, LANES), lambda h, p, qi, ki: (qi[p], cq + h)),
                  pl.BlockSpec((tk, LANES), lambda h, p, qi, ki: (ki[p], h)),
                  pl.BlockSpec((LANES, tk), lambda h, p, qi, ki: (h, ki[p])),
                  pl.BlockSpec(lam_p.shape, lambda h, p, qi, ki: (0, 0)),
                  pl.BlockSpec((1, LANES), lambda h, p, qi, ki: (0, 0))],
        out_specs=pl.BlockSpec((tq, LANES), lambda h, p, qi, ki: (qi[p], h)),
        scratch_shapes=[pltpu.VMEM((LANES, 2 * tq), BF16),
                        pltpu.VMEM((1, 2 * tq), F32),
                        pltpu.VMEM((1, 2 * tq), F32),
                        pltpu.VMEM((LANES, 2 * tq), F32)])
    return pl.pallas_call(
        functools.partial(_flash_kernel, tq=tq, tk=tk, lambda_init=lambda_init),
        out_shape=jax.ShapeDtypeStruct((t, n_heads * LANES), BF16),
        grid_spec=grid_spec,
        compiler_params=_cparams("parallel", "arbitrary"), name="flash_attn",
    )(qi, ki, proj, k_b, v_t, lam_p, subln.reshape(1, LANES))


def _cached_attn_kernel(q_ref, kn_ref, vn_ref, kc_ref, vc_ref, lam_ref, g_ref, o_ref, *, lambda_init):
    ln = q_ref.shape[0]
    past = kc_ref.shape[0]
    qs = _stack_maps(q_ref[...] * (float(CHUNK) ** -0.5))
    sc = _dot_nt(qs, kc_ref[...].astype(BF16))
    sn = _dot_nt(qs, kn_ref[...].astype(BF16))
    q_chunk_c = (past + lax.broadcasted_iota(I32, sc.shape, 0) % ln) // CHUNK
    q_chunk_n = (past + lax.broadcasted_iota(I32, sn.shape, 0) % ln) // CHUNK
    sc = jnp.where(lax.broadcasted_iota(I32, sc.shape, 1) // CHUNK <= q_chunk_c, sc, -jnp.inf)
    sn = jnp.where((past + lax.broadcasted_iota(I32, sn.shape, 1)) // CHUNK <= q_chunk_n, sn, -jnp.inf)
    m = jnp.maximum(jnp.max(sc, axis=1, keepdims=True), jnp.max(sn, axis=1, keepdims=True))
    pc = jnp.exp(sc - m)
    pn = jnp.exp(sn - m)
    denom = jnp.sum(pc, axis=1, keepdims=True) + jnp.sum(pn, axis=1, keepdims=True)
    o = (_dot(pc.astype(BF16), vc_ref[...].astype(BF16)) + _dot(pn.astype(BF16), vn_ref[...].astype(BF16))) / denom
    lam = _lambda_value(lam_ref, lambda_init)
    o_ref[...] = _diff_finish(o[0:ln], o[ln:2 * ln], lam, g_ref[...], lambda_init).astype(o_ref.dtype)


def _cached_attention(proj3, col_q, col_k, col_v, n_heads, k_cache, v_cache, lam_p, subln, lambda_init):
    b, ln, _ = proj3.shape
    past = k_cache.shape[1]
    cq, ck, cv = col_q // LANES, col_k // LANES, col_v // LANES
    new = lambda c: pl.BlockSpec((None, ln, LANES), lambda bi, h: (bi, 0, c + h))
    old = pl.BlockSpec((None, past, LANES), lambda bi, h: (bi, 0, h))
    return pl.pallas_call(
        functools.partial(_cached_attn_kernel, lambda_init=lambda_init),
        out_shape=jax.ShapeDtypeStruct((b, ln, n_heads * LANES), BF16),
        grid=(b, n_heads),
        in_specs=[new(cq), new(ck), new(cv), old, old,
                  pl.BlockSpec(lam_p.shape, lambda bi, h: (0, 0)),
                  pl.BlockSpec((1, LANES), lambda bi, h: (0, 0))],
        out_specs=pl.BlockSpec((None, ln, LANES), lambda bi, h: (bi, 0, h)),
        compiler_params=_cparams("parallel", "parallel"), name="cached_attn",
    )(proj3, proj3, proj3, k_cache, v_cache, lam_p, subln.reshape(1, LANES))


def _split3(x):
    x1 = x.astype(BF16)
    r1 = x - x1.astype(F32)
    x2 = r1.astype(BF16)
    x3 = (r1 - x2.astype(F32)).astype(BF16)
    return x1, x2, x3


def _ssd_kernel(xs_ref, bm_ref, cm_ref, hx_ref, hb_ref, hc_ref, sx_ref, sb_ref, sc_ref,
                wx_ref, wb_ref, wc_ref, bx_ref, bb_ref, bc_ref,
                dt_ref, dtb_ref, alog_ref, z_ref, dexp_ref, gn_ref, h0_ref,
                y_ref, hout_ref, nsx_ref, nsb_ref, nsc_ref,
                scr_x, scr_b, scr_c, h_scr, y_scr, xte_scr,
                *, q, qp, n_heads, p_dim, n_state, n_groups, width):
    i = pl.program_id(1)
    first = i == 0
    halo = SUBLANES

    @pl.when(first)
    def _():
        h_scr[...] = h0_ref[...]

    def conv(t_ref, halo_ref, st_ref, w_ref, b_ref, scr, ns_ref):
        scr[0:halo] = jnp.where(first, st_ref[...], halo_ref[...])
        scr[halo:halo + q] = t_ref[...]
        acc = b_ref[...] + w_ref[0:1, :] * scr[halo - width + 1:halo - width + 1 + q]
        for k in range(1, width):
            acc = acc + w_ref[k:k + 1, :] * scr[halo - width + 1 + k:halo - width + 1 + k + q]
        ns_ref[...] = scr[q:q + halo]
        return _silu(acc)

    xs = conv(xs_ref, hx_ref, sx_ref, wx_ref, bx_ref, scr_x, nsx_ref)
    bm = conv(bm_ref, hb_ref, sb_ref, wb_ref, bb_ref, scr_b, nsb_ref)
    cm = conv(cm_ref, hc_ref, sc_ref, wc_ref, bc_ref, scr_c, nsc_ref)
    dt = jax.nn.softplus(dt_ref[...] + dtb_ref[...])
    a = -jnp.exp(alog_ref[...])

    def pad(v):
        if qp == q:
            return v
        return jnp.concatenate([v, jnp.zeros((qp - q, v.shape[1]), v.dtype)], axis=0)

    xs_p, bm_p, cm_p, dt_p = pad(xs), pad(bm), pad(cm), pad(dt)
    d_a = dt_p * a
    row = lax.broadcasted_iota(I32, (qp, qp), 0)
    col = lax.broadcasted_iota(I32, (qp, qp), 1)
    tril = row >= col
    ones_tril = jnp.where(tril, 1.0, 0.0).astype(BF16)
    d1, d2, d3 = _split3(d_a)
    acum = _dot(ones_tril, d1) + _dot(ones_tril, d2) + _dot(ones_tril, d3)
    acum_t = acum.T
    total = acum[qp - 1:qp, :]
    bm_b = bm_p.astype(BF16)
    cm_b = cm_p.astype(BF16)
    scores = [_dot_nt(cm_b[:, g * n_state:(g + 1) * n_state], bm_b[:, g * n_state:(g + 1) * n_state])
              for g in range(n_groups)]
    rep = n_heads // n_groups
    for h in range(n_heads):
        g = h // rep
        colv = acum[:, h:h + 1]
        rowv = acum_t[h:h + 1, :]
        tot = total[:, h:h + 1]
        decay = jnp.where(tril, jnp.exp(colv - rowv), 0.0)
        xdt = xs_p[:, h * p_dim:(h + 1) * p_dim] * dt_p[:, h:h + 1]
        y_diag = _dot((scores[g] * decay).astype(BF16), xdt.astype(BF16))
        h_prev = h_scr[h]
        y_off = jnp.exp(colv) * _dot_nt(cm_b[:, g * n_state:(g + 1) * n_state], h_prev.astype(BF16))
        y_scr[:, h * p_dim:(h + 1) * p_dim] = y_diag + y_off
        xte_scr[:, h * p_dim:(h + 1) * p_dim] = xdt * jnp.exp(tot - colv)
    xte_t = xte_scr[...].T.astype(BF16)
    for h in range(n_heads):
        g = h // rep
        st = _dot(xte_t[h * p_dim:(h + 1) * p_dim, :], bm_b[:, g * n_state:(g + 1) * n_state])
        h_scr[h] = jnp.exp(total[:, h:h + 1]) * h_scr[h] + st
    y = y_scr[0:q] + dexp_ref[...] * xs
    yg = y * _silu(z_ref[...])
    ms = jnp.mean(yg * yg, axis=-1, keepdims=True)
    y_ref[...] = (yg * lax.rsqrt(ms + EPS) * gn_ref[...]).astype(y_ref.dtype)
    hout_ref[...] = h_scr[...]


def _ssd(proj3, cols, dims, conv_state, conv_w, conv_b, dt_bias, a_log, d_skip, norm_g, h0):
    b, ln, _ = proj3.shape
    n_heads, p_dim, n_state, n_groups, width = dims
    d_ssd = n_heads * p_dim
    gn = n_groups * n_state
    q = _tile(ln, SSD_CHUNK)
    qp = max(q, LANES)
    assert q % SUBLANES == 0 and ln >= width - 1 and width - 1 <= SUBLANES and n_heads <= LANES
    nsteps = ln // q
    halo = SUBLANES
    widths = (d_ssd, gn, gn)
    offs = (cols["xs"], cols["bm"], cols["cm"])
    for o, w in zip(offs, widths):
        assert o % w == 0
    assert cols["z"] % d_ssd == 0 and cols["dt"] % LANES == 0

    def tile_spec(o, w):
        return pl.BlockSpec((None, q, w), lambda bi, i: (bi, i, o // w))

    def halo_spec(o, w):
        return pl.BlockSpec((None, halo, w), lambda bi, i: (bi, jnp.maximum(i * (q // halo) - 1, 0), o // w))

    def state_spec(w):
        return pl.BlockSpec((None, halo, w), lambda bi, i: (bi, 0, 0))

    def const_spec(shape):
        return pl.BlockSpec(shape, lambda bi, i: (0,) * len(shape))

    st = jnp.pad(conv_state, ((0, 0), (halo - (width - 1), 0), (0, 0)))
    bounds = (0, d_ssd, d_ssd + gn, d_ssd + 2 * gn)
    st_segs = [st[..., bounds[k]:bounds[k + 1]] for k in range(3)]
    w_segs = [conv_w[:, bounds[k]:bounds[k + 1]] for k in range(3)]
    b_segs = [conv_b[bounds[k]:bounds[k + 1]].reshape(1, -1) for k in range(3)]
    pad_lane = lambda v: jnp.pad(v.reshape(1, -1), ((0, 0), (0, LANES - v.shape[0])))
    in_specs = ([tile_spec(o, w) for o, w in zip(offs, widths)]
                + [halo_spec(o, w) for o, w in zip(offs, widths)]
                + [state_spec(w) for w in widths]
                + [const_spec((width, w)) for w in widths]
                + [const_spec((1, w)) for w in widths]
                + [tile_spec(cols["dt"], LANES), const_spec((1, LANES)), const_spec((1, LANES)),
                   tile_spec(cols["z"], d_ssd), const_spec((1, d_ssd)), const_spec((1, d_ssd)),
                   pl.BlockSpec((None, n_heads, p_dim, n_state), lambda bi, i: (bi, 0, 0, 0))])
    out_shape = (jax.ShapeDtypeStruct((b, ln, d_ssd), BF16),
                 jax.ShapeDtypeStruct((b, n_heads, p_dim, n_state), F32),
                 jax.ShapeDtypeStruct((b, halo, d_ssd), F32),
                 jax.ShapeDtypeStruct((b, halo, gn), F32),
                 jax.ShapeDtypeStruct((b, halo, gn), F32))
    out_specs = (pl.BlockSpec((None, q, d_ssd), lambda bi, i: (bi, i, 0)),
                 pl.BlockSpec((None, n_heads, p_dim, n_state), lambda bi, i: (bi, 0, 0, 0)),
                 state_spec(d_ssd), state_spec(gn), state_spec(gn))
    scratch = [pltpu.VMEM((q + halo, d_ssd), F32), pltpu.VMEM((q + halo, gn), F32), pltpu.VMEM((q + halo, gn), F32),
               pltpu.VMEM((n_heads, p_dim, n_state), F32),
               pltpu.VMEM((qp, d_ssd), F32), pltpu.VMEM((qp, d_ssd), F32)]
    y, h_new, nsx, nsb, nsc = pl.pallas_call(
        functools.partial(_ssd_kernel, q=q, qp=qp, n_heads=n_heads, p_dim=p_dim, n_state=n_state,
                          n_groups=n_groups, width=width),
        out_shape=out_shape, grid=(b, nsteps), in_specs=in_specs, out_specs=out_specs,
        scratch_shapes=scratch,
        compiler_params=_cparams("parallel", "arbitrary"), name="ssd",
    )(proj3, proj3, proj3, proj3, proj3, proj3, *st_segs, *w_segs, *b_segs,
      proj3, pad_lane(dt_bias), pad_lane(a_log), proj3,
      jnp.repeat(d_skip, p_dim).reshape(1, d_ssd), norm_g.reshape(1, d_ssd), h0)
    new_state = jnp.concatenate([nsx, nsb, nsc], axis=-1)[:, halo - (width - 1):, :]
    return y, h_new, new_state


def _conf_kernel(ga_ref, gg_ref, hga_ref, hgg_ref, st_ref, w_ref, b_ref, lg_ref, lb_ref,
                 y_ref, ns_ref, scr, *, tb, width, halo):
    i = pl.program_id(1)
    u = ga_ref[...] * jax.nn.sigmoid(gg_ref[...])
    uh = hga_ref[...] * jax.nn.sigmoid(hgg_ref[...])
    scr[0:halo] = jnp.where(i == 0, st_ref[...], uh)
    scr[halo:halo + tb] = u
    base = halo - (width - 1)
    acc = b_ref[...] + w_ref[0:1, :] * scr[base:base + tb]
    for k in range(1, width):
        acc = acc + w_ref[k:k + 1, :] * scr[base + k:base + k + tb]
    mu = jnp.mean(acc, axis=-1, keepdims=True)
    cen = acc - mu
    var = jnp.mean(cen * cen, axis=-1, keepdims=True)
    y = cen * lax.rsqrt(var + EPS) * lg_ref[...] + lb_ref[...]
    y_ref[...] = _silu(y).astype(y_ref.dtype)
    ns_ref[...] = scr[tb:tb + halo]


def _conf(proj3, col_ga, col_gg, d_conv, conv_state, w, bias, ln_g, ln_b):
    b, ln, _ = proj3.shape
    width = w.shape[0]
    halo = -(-(width - 1) // SUBLANES) * SUBLANES
    tb = _tile(ln, CONF_TILE)
    assert tb % halo == 0 and ln >= width - 1 and col_ga % d_conv == 0 and col_gg % d_conv == 0
    st = jnp.pad(conv_state, ((0, 0), (halo - (width - 1), 0), (0, 0)))
    tile_spec = lambda o: pl.BlockSpec((None, tb, d_conv), lambda bi, i: (bi, i, o // d_conv))
    halo_spec = lambda o: pl.BlockSpec((None, halo, d_conv),
                                       lambda bi, i: (bi, jnp.maximum(i * (tb // halo) - 1, 0), o // d_conv))
    st_spec = pl.BlockSpec((None, halo, d_conv), lambda bi, i: (bi, 0, 0))
    vec = pl.BlockSpec((1, d_conv), lambda bi, i: (0, 0))
    y, ns = pl.pallas_call(
        functools.partial(_conf_kernel, tb=tb, width=width, halo=halo),
        out_shape=(jax.ShapeDtypeStruct((b, ln, d_conv), BF16), jax.ShapeDtypeStruct((b, halo, d_conv), F32)),
        grid=(b, ln // tb),
        in_specs=[tile_spec(col_ga), tile_spec(col_gg), halo_spec(col_ga), halo_spec(col_gg), st_spec,
                  pl.BlockSpec((width, d_conv), lambda bi, i: (0, 0)), vec, vec, vec],
        out_specs=(pl.BlockSpec((None, tb, d_conv), lambda bi, i: (bi, i, 0)), st_spec),
        scratch_shapes=[pltpu.VMEM((tb + halo, d_conv), F32)],
        compiler_params=_cparams("parallel", "arbitrary"), name="conf_conv",
    )(proj3, proj3, proj3, proj3, st, w, bias.reshape(1, -1), ln_g.reshape(1, -1), ln_b.reshape(1, -1))
    return y, ns[:, halo - (width - 1):, :]


def _outproj_kernel(ya_ref, yb_ref, yc_ref, wa_ref, wb_ref, wc_ref, res_ref, gate_ref, o_ref):
    acc = _dot(ya_ref[...], wa_ref[...]) + _dot(yb_ref[...], wb_ref[...]) + _dot(yc_ref[...], wc_ref[...])
    o_ref[...] = res_ref[...] + gate_ref[...] * acc


def _outproj(ya, yb, yc, w_out, res, gate):
    t, d = res.shape
    da, db = ya.shape[1], yb.shape[1]
    tm = _tile(t, ROW_TILE)
    tn = _tile(d, 1024)
    a_spec = lambda k: pl.BlockSpec((tm, k), lambda j, i: (i, 0))
    w_spec = lambda k: pl.BlockSpec((k, tn), lambda j, i: (0, j))
    return pl.pallas_call(
        _outproj_kernel, out_shape=jax.ShapeDtypeStruct((t, d), F32),
        grid=(d // tn, t // tm),
        in_specs=[a_spec(da), a_spec(db), a_spec(yc.shape[1]), w_spec(da), w_spec(db), w_spec(yc.shape[1]),
                  pl.BlockSpec((tm, tn), lambda j, i: (i, j)), _rows_spec(gate, tm, tn, by_col=True)],
        out_specs=pl.BlockSpec((tm, tn), lambda j, i: (i, j)),
        compiler_params=_cparams("parallel", "parallel"), name="outproj",
    )(ya, yb, yc, w_out[:da], w_out[da:da + db], w_out[da + db:], res, gate)


def _ffn_up_kernel(a_ref, w1_ref, w3_ref, o_ref):
    a = a_ref[...]
    o_ref[...] = (_silu(_dot(a, w1_ref[...])) * _dot(a, w3_ref[...])).astype(o_ref.dtype)


def _ffn_down_kernel(a_ref, w_ref, res_ref, gate_ref, o_ref):
    o_ref[...] = res_ref[...] + gate_ref[...] * _dot(a_ref[...], w_ref[...])


def _ffn(h, w1, w3, w2, res, gate):
    t, d = h.shape
    f = w1.shape[1]
    tm = _tile(t, ROW_TILE)
    tn = _tile(f, 512)
    up = pl.pallas_call(
        _ffn_up_kernel, out_shape=jax.ShapeDtypeStruct((t, f), BF16),
        grid=(f // tn, t // tm),
        in_specs=[pl.BlockSpec((tm, d), lambda j, i: (i, 0)),
                  pl.BlockSpec((d, tn), lambda j, i: (0, j)),
                  pl.BlockSpec((d, tn), lambda j, i: (0, j))],
        out_specs=pl.BlockSpec((tm, tn), lambda j, i: (i, j)),
        compiler_params=_cparams("parallel", "parallel"), name="ffn_up",
    )(h, w1, w3)
    tn2 = _tile(d, 512)
    return pl.pallas_call(
        _ffn_down_kernel, out_shape=jax.ShapeDtypeStruct((t, d), F32),
        grid=(d // tn2, t // tm),
        in_specs=[pl.BlockSpec((tm, f), lambda j, i: (i, 0)),
                  pl.BlockSpec((f, tn2), lambda j, i: (0, j)),
                  pl.BlockSpec((tm, tn2), lambda j, i: (i, j)),
                  _rows_spec(gate, tm, tn2, by_col=True)],
        out_specs=pl.BlockSpec((tm, tn2), lambda j, i: (i, j)),
        compiler_params=_cparams("parallel", "parallel"), name="ffn_down",
    )(up, w2, res, gate)


def _router_kernel(x_ref, g_ref, sc_ref, sh_ref, r_ref, hp_ref, sel_ref, gate_ref, tot_ref, *, n_experts):
    h = _modulated(x_ref[...], g_ref[...], sc_ref[...], sh_ref[...])
    half = h.shape[1] // 2
    lo = lax.bitcast_convert_type(h[:, :half].astype(BF16).astype(F32), U32)
    hi = lax.bitcast_convert_type(h[:, half:].astype(BF16).astype(F32), U32)
    hp_ref[...] = (lo >> 16) | hi
    r = r_ref[...]
    h1 = h.astype(BF16)
    h2 = (h - h1.astype(F32)).astype(BF16)
    r1 = r.astype(BF16)
    r2 = (r - r1.astype(F32)).astype(BF16)
    logits = _dot(h1, r1) + _dot(h1, r2) + _dot(h2, r1)
    lane = lax.broadcasted_iota(I32, logits.shape, 1)
    lane_f = lane.astype(F32)
    lg = jnp.where(lane < n_experts, logits, -jnp.inf)
    v1 = jnp.max(lg, axis=1, keepdims=True)
    i1 = jnp.min(jnp.where(lg == v1, lane_f, float(LANES)), axis=1, keepdims=True)
    lg2 = jnp.where(lane_f == i1, -jnp.inf, lg)
    v2 = jnp.max(lg2, axis=1, keepdims=True)
    i2 = jnp.min(jnp.where(lg2 == v2, lane_f, float(LANES)), axis=1, keepdims=True)
    e = jnp.exp(v2 - v1)
    g1 = 1.0 / (1.0 + e)
    g2 = e * g1
    first = lane_f == i1
    second = lane_f == i2
    sel_ref[...] = jnp.where(first, 1, jnp.where(second, 2, 0)).astype(I32)
    gate_ref[...] = jnp.where(lane == 0, g1, jnp.where(lane == 1, g2, 0.0))
    cnt = jnp.sum(jnp.where(first | second, 1.0, 0.0), axis=0, keepdims=True)

    @pl.when(pl.program_id(1) == 0)
    def _():
        tot_ref[...] = jnp.zeros(tot_ref.shape, F32)

    tot_ref[...] += cnt


def _router(x, g, sc, sh, router_w):
    t, d = x.shape
    n_experts = router_w.shape[1]
    tm = _tile(t, ROW_TILE)
    r = jnp.pad(router_w, ((0, 0), (0, LANES - n_experts)))
    row = lambda w: pl.BlockSpec((tm, w), lambda j, i: (i, 0))
    return pl.pallas_call(
        functools.partial(_router_kernel, n_experts=n_experts),
        out_shape=(jax.ShapeDtypeStruct((t, d // 2), U32), jax.ShapeDtypeStruct((t, LANES), I32),
                   jax.ShapeDtypeStruct((t, LANES), F32), jax.ShapeDtypeStruct((1, LANES), F32)),
        grid=(1, t // tm),
        in_specs=[row(d), pl.BlockSpec((1, d), lambda j, i: (0, 0)), _rows_spec(sc, tm, d), _rows_spec(sh, tm, d),
                  pl.BlockSpec((d, LANES), lambda j, i: (0, 0))],
        out_specs=(row(d // 2), row(LANES), row(LANES), pl.BlockSpec((1, LANES), lambda j, i: (0, 0))),
        compiler_params=_cparams("arbitrary", "arbitrary"), name="router",
    )(x, g.reshape(1, d), sc, sh, r)


def _route_kernel(sel_ref, off_ref, pos_ref, carry_ref):
    @pl.when(pl.program_id(0) == 0)
    def _():
        carry_ref[...] = jnp.zeros(carry_ref.shape, F32)

    s = sel_ref[...]
    tm = s.shape[0]
    cnt = jnp.where(s > 0, 1.0, 0.0)
    row = lax.broadcasted_iota(I32, (tm, tm), 0)
    col = lax.broadcasted_iota(I32, (tm, tm), 1)
    before = jnp.where(row > col, 1.0, 0.0).astype(BF16)
    dest = _dot(before, cnt.astype(BF16)) + carry_ref[...] + off_ref[...]
    p1 = jnp.sum(jnp.where(s == 1, dest, 0.0), axis=1, keepdims=True)
    p2 = jnp.sum(jnp.where(s == 2, dest, 0.0), axis=1, keepdims=True)
    lane = lax.broadcasted_iota(I32, s.shape, 1)
    pos_ref[...] = jnp.where(lane == 0, p1, jnp.where(lane == 1, p2, 0.0)).astype(I32)
    carry_ref[...] += jnp.sum(cnt, axis=0, keepdims=True)


def _route(sel, offsets):
    t = sel.shape[0]
    tm = _tile(t, ROW_TILE)
    return pl.pallas_call(
        _route_kernel, out_shape=jax.ShapeDtypeStruct((t, LANES), I32),
        grid=(t // tm,),
        in_specs=[pl.BlockSpec((tm, LANES), lambda i: (i, 0)), pl.BlockSpec((1, LANES), lambda i: (0, 0))],
        out_specs=pl.BlockSpec((tm, LANES), lambda i: (i, 0)),
        scratch_shapes=[pltpu.VMEM((1, LANES), F32)],
        compiler_params=_cparams("arbitrary"), name="route",
    )(sel, offsets)


def _dispatch_kernel(p1_ref, p2_ref, hp_ref, init_ref, xs_ref, sem, *, tr):
    del init_ref
    base = pl.program_id(0) * tr

    def copy(r, dst):
        return pltpu.make_async_copy(hp_ref.at[pl.ds(r, 1), :], xs_ref.at[pl.ds(dst, 1), :], sem)

    def issue(r, c):
        copy(r, p1_ref[base + r]).start()
        copy(r, p2_ref[base + r]).start()
        return c

    def drain(r, c):
        copy(r, p1_ref[base + r]).wait()
        copy(r, p2_ref[base + r]).wait()
        return c

    lax.fori_loop(0, tr, issue, 0)
    lax.fori_loop(0, tr, drain, 0)


def _dispatch(hp, pos1, pos2, n_rows):
    t, w = hp.shape
    tr = _tile(t, GATHER_TILE)
    grid_spec = pltpu.PrefetchScalarGridSpec(
        num_scalar_prefetch=2, grid=(t // tr,),
        in_specs=[pl.BlockSpec((tr, w), lambda i, p1, p2: (i, 0)), pl.BlockSpec(memory_space=pl.ANY)],
        out_specs=pl.BlockSpec(memory_space=pl.ANY),
        scratch_shapes=[pltpu.SemaphoreType.DMA(())])
    return pl.pallas_call(
        functools.partial(_dispatch_kernel, tr=tr),
        out_shape=jax.ShapeDtypeStruct((n_rows, w), U32), grid_spec=grid_spec,
        input_output_aliases={3: 0},
        compiler_params=_cparams("arbitrary"), name="dispatch",
    )(pos1, pos2, hp, jnp.zeros((n_rows, w), U32))


def _unpack(u):
    lo = lax.bitcast_convert_type(u << 16, F32).astype(BF16)
    hi = lax.bitcast_convert_type(u & jnp.uint32(0xFFFF0000), F32).astype(BF16)
    return lo, hi


def _new_expert(te_ref, i):
    return jnp.logical_or(i == 0, te_ref[i] != te_ref[jnp.maximum(i - 1, 0)])


def _moe_up_kernel(te_ref, nv_ref, xs_ref, w1_ref, w3_ref, o_ref, w1b_ref, w3b_ref):
    i = pl.program_id(1)
    live = i < nv_ref[0]

    @pl.when(jnp.logical_and(live, _new_expert(te_ref, i)))
    def _():
        w1b_ref[...] = w1_ref[...].astype(BF16)
        w3b_ref[...] = w3_ref[...].astype(BF16)

    @pl.when(live)
    def _():
        lo, hi = _unpack(xs_ref[...])
        half = lo.shape[1]
        a = _dot(lo, w1b_ref[0:half, :]) + _dot(hi, w1b_ref[half:2 * half, :])
        b = _dot(lo, w3b_ref[0:half, :]) + _dot(hi, w3b_ref[half:2 * half, :])
        o_ref[...] = (_silu(a) * b).astype(o_ref.dtype)

    @pl.when(jnp.logical_not(live))
    def _():
        o_ref[...] = jnp.zeros(o_ref.shape, o_ref.dtype)


def _moe_down_kernel(te_ref, nv_ref, a_ref, w_ref, o_ref, wb_ref):
    i = pl.program_id(1)
    live = i < nv_ref[0]

    @pl.when(jnp.logical_and(live, _new_expert(te_ref, i)))
    def _():
        wb_ref[...] = w_ref[...].astype(BF16)

    @pl.when(live)
    def _():
        o_ref[...] = _dot(a_ref[...], wb_ref[...])

    @pl.when(jnp.logical_not(live))
    def _():
        o_ref[...] = jnp.zeros(o_ref.shape, o_ref.dtype)


def _moe_experts(xs, tile_expert, n_valid, w1, w3, w2, tm, li):
    n_rows, half = xs.shape
    _, _, d, f = w1.shape
    n_tiles = n_rows // tm
    tn = _tile(f, 512)
    live = lambda i, nv: jnp.minimum(i, nv[0] - 1)
    up = pl.pallas_call(
        _moe_up_kernel, out_shape=jax.ShapeDtypeStruct((n_rows, f), BF16),
        grid_spec=pltpu.PrefetchScalarGridSpec(
            num_scalar_prefetch=2, grid=(f // tn, n_tiles),
            in_specs=[pl.BlockSpec((tm, half), lambda j, i, te, nv: (live(i, nv), 0)),
                      pl.BlockSpec((None, None, d, tn), lambda j, i, te, nv: (li, te[live(i, nv)], 0, j)),
                      pl.BlockSpec((None, None, d, tn), lambda j, i, te, nv: (li, te[live(i, nv)], 0, j))],
            out_specs=pl.BlockSpec((tm, tn), lambda j, i, te, nv: (i, j)),
            scratch_shapes=[pltpu.VMEM((d, tn), BF16), pltpu.VMEM((d, tn), BF16)]),
        compiler_params=_cparams("parallel", "arbitrary"), name="moe_up",
    )(tile_expert, n_valid, xs, w1, w3)
    split = 2 if tm % (4 * SUBLANES) == 0 and tm >= 256 else 1
    tm2 = tm // split
    te2 = jnp.repeat(tile_expert, split)
    nv2 = n_valid * split
    tn2 = _tile(d, 512)
    return pl.pallas_call(
        _moe_down_kernel, out_shape=jax.ShapeDtypeStruct((n_rows, d), F32),
        grid_spec=pltpu.PrefetchScalarGridSpec(
            num_scalar_prefetch=2, grid=(d // tn2, n_tiles * split),
            in_specs=[pl.BlockSpec((tm2, f), lambda j, i, te, nv: (live(i, nv), 0)),
                      pl.BlockSpec((None, None, f, tn2), lambda j, i, te, nv: (li, te[live(i, nv)], 0, j))],
            out_specs=pl.BlockSpec((tm2, tn2), lambda j, i, te, nv: (i, j)),
            scratch_shapes=[pltpu.VMEM((f, tn2), BF16)]),
        compiler_params=_cparams("parallel", "arbitrary"), name="moe_down",
    )(te2, nv2, up, w2)


def _combine_kernel(p1_ref, p2_ref, x_ref, gates_ref, gmod_ref, y_ref, o_ref, buf1, buf2, sem, *, tr):
    base = pl.program_id(0) * tr

    def copy(src, buf, r):
        return pltpu.make_async_copy(y_ref.at[pl.ds(src, 1), :], buf.at[pl.ds(r, 1), :], sem)

    def issue(r, c):
        copy(p1_ref[base + r], buf1, r).start()
        copy(p2_ref[base + r], buf2, r).start()
        return c

    def drain(r, c):
        copy(p1_ref[base + r], buf1, r).wait()
        copy(p2_ref[base + r], buf2, r).wait()
        return c

    lax.fori_loop(0, tr, issue, 0)
    lax.fori_loop(0, tr, drain, 0)
    g = gates_ref[...]
    f = g[:, 0:1] * buf1[...] + g[:, 1:2] * buf2[...]
    o_ref[...] = x_ref[...] + gmod_ref[...] * f


def _combine(x, gates, gmod, y, pos1, pos2):
    t, d = x.shape
    tr = _tile(t, GATHER_TILE)
    gspec = (pl.BlockSpec((1, d), lambda i, p1, p2: (0, 0)) if gmod.shape[0] == 1
             else pl.BlockSpec((tr, d), lambda i, p1, p2: (i, 0)))
    grid_spec = pltpu.PrefetchScalarGridSpec(
        num_scalar_prefetch=2, grid=(t // tr,),
        in_specs=[pl.BlockSpec((tr, d), lambda i, p1, p2: (i, 0)),
                  pl.BlockSpec((tr, LANES), lambda i, p1, p2: (i, 0)),
                  gspec, pl.BlockSpec(memory_space=pl.ANY)],
        out_specs=pl.BlockSpec((tr, d), lambda i, p1, p2: (i, 0)),
        scratch_shapes=[pltpu.VMEM((tr, d), F32), pltpu.VMEM((tr, d), F32), pltpu.SemaphoreType.DMA(())])
    return pl.pallas_call(
        functools.partial(_combine_kernel, tr=tr),
        out_shape=jax.ShapeDtypeStruct((t, d), F32), grid_spec=grid_spec,
        compiler_params=_cparams("arbitrary"), name="combine",
    )(pos1, pos2, x, gates, gmod, y)


def _moe(x, g, sc, sh, gmod, router_w, w1, w3, w2, li):
    t, d = x.shape
    n_experts = router_w.shape[1]
    tm = min(EXPERT_TILE, max(SUBLANES * 2, (2 * t) // n_experts))
    n_tiles = -(-2 * t // tm) + n_experts
    hp, sel, gates, totals = _router(x, g, sc, sh, router_w)
    counts = totals[0, :n_experts].astype(I32)
    tiles_per = (counts + tm - 1) // tm
    tile_end = jnp.cumsum(tiles_per)
    offsets = jnp.pad(((tile_end - tiles_per) * tm).astype(F32), (0, LANES - n_experts)).reshape(1, LANES)
    tile_expert = jnp.minimum(jnp.sum(jnp.arange(n_tiles, dtype=I32)[:, None] >= tile_end[None, :], axis=1),
                              n_experts - 1).astype(I32)
    n_valid = tile_end[n_experts - 1:].astype(I32)
    pos = _route(sel, offsets)
    pos1, pos2 = pos[:, 0], pos[:, 1]
    xs = _dispatch(hp, pos1, pos2, n_tiles * tm)
    y = _moe_experts(xs, tile_expert, n_valid, w1, w3, w2, tm, li)
    return _combine(x, gates, gmod, y, pos1, pos2)


def _rope_tables(pos):
    half = CHUNK // 2
    inv = ROPE_THETA ** (-jnp.arange(half, dtype=F32) / half)
    ang = pos.astype(F32)[:, None] * inv[None, :]
    cos, sin = jnp.cos(ang), jnp.sin(ang)
    reps = LANES // CHUNK
    return jnp.tile(cos, (1, 2 * reps)), jnp.tile(jnp.concatenate([-sin, sin], axis=1), (1, reps))


def _lambda_init(layer):
    return 0.8 - 0.6 * math.exp(-0.3 * layer)


def _layout(d_attn, d_ssd, gn, n_heads_b, d_conv):
    src = {"q": 0, "k": d_attn, "v": 2 * d_attn, "z": 3 * d_attn, "xs": 3 * d_attn + d_ssd,
           "bm": 3 * d_attn + 2 * d_ssd, "cm": 3 * d_attn + 2 * d_ssd + gn,
           "dt": 3 * d_attn + 2 * d_ssd + 2 * gn, "ga": 3 * d_attn + 2 * d_ssd + 2 * gn + n_heads_b,
           "gg": 3 * d_attn + 2 * d_ssd + 2 * gn + n_heads_b + d_conv}
    width = {"q": d_attn, "k": d_attn, "v": d_attn, "z": d_ssd, "xs": d_ssd, "bm": gn, "cm": gn,
             "dt": n_heads_b, "ga": d_conv, "gg": d_conv}
    order = ["q", "k", "v", "xs", "bm", "cm", "z", "ga", "gg", "dt"]
    cols, off = {}, 0
    for name in order:
        cols[name] = off
        off += LANES if name == "dt" else width[name]
    tn = min(1024, d_attn)
    total = -(-off // tn) * tn
    return src, width, order, cols, total, tn


def _permute_w_in(w, src, width, order, total):
    parts = []
    for name in order:
        blk = w[:, src[name]:src[name] + width[name]]
        if name == "dt":
            blk = jnp.pad(blk, ((0, 0), (0, LANES - width[name])))
        parts.append(blk)
    out = jnp.concatenate(parts, axis=1)
    return jnp.pad(out, ((0, 0), (0, total - out.shape[1]))).astype(BF16)


def kernel(x_prompt, x_sample, c_prompt, c_sample, cache_k, cache_v, state_ssm, state_conv_ssd, state_conv_conf,
           w_ada, b_ada, norm_mix, norm_ffn, w_in, w_out, attn_lambda, attn_subln, ssd_conv_w, ssd_conv_b,
           ssd_dt_bias, ssd_a_log, ssd_d, ssd_norm, conf_dw_w, conf_dw_b, conf_ln_g, conf_ln_b,
           ffn_w1, ffn_w3, ffn_w2, moe_router, moe_w1, moe_w3, moe_w2, norm_final):
    bp, lp, d = x_prompt.shape
    bs, ls, _ = x_sample.shape
    depth = w_in.shape[0]
    past = cache_k.shape[2]
    n_heads_a, head_dim = cache_k.shape[3], cache_k.shape[5]
    assert 2 * head_dim == LANES and head_dim == CHUNK and bp == 1
    d_attn = n_heads_a * 2 * head_dim
    n_heads_b, p_dim, n_state = state_ssm.shape[2:]
    d_ssd = n_heads_b * p_dim
    conv_dim_b = state_conv_ssd.shape[3]
    gn = (conv_dim_b - d_ssd) // 2
    n_groups = gn // n_state
    d_conv = state_conv_conf.shape[3]
    ssd_dims = (n_heads_b, p_dim, n_state, n_groups, ssd_conv_w.shape[1])
    src, width, order, cols, n_total, tn_proj = _layout(d_attn, d_ssd, gn, n_heads_b, d_conv)
    n_rope_blocks = 2 * d_attn // tn_proj

    tp, ts = bp * lp, bs * ls
    pos_p = jnp.arange(lp, dtype=I32)
    pos_s = past + jnp.arange(ls, dtype=I32)
    cos_p, sin_p = _rope_tables(pos_p)
    cos_s, sin_s = _rope_tables(jnp.tile(pos_s, bs))

    c_all = jnp.concatenate([c_prompt, c_sample], axis=0)
    c_rows = -(-c_all.shape[0] // (2 * SUBLANES)) * (2 * SUBLANES)
    mods = _adaln(jnp.pad(c_all, ((0, c_rows - c_all.shape[0]), (0, 0))), w_ada, b_ada)

    xp = x_prompt.reshape(tp, d)
    xs = x_sample.reshape(ts, d)
    zeros_ssm = jnp.zeros((bp, n_heads_b, p_dim, n_state), F32)
    zeros_cb = jnp.zeros((bp, ssd_conv_w.shape[1] - 1, conv_dim_b), F32)
    zeros_cc = jnp.zeros((bp, conf_dw_w.shape[1] - 1, d_conv), F32)
    outs_p, outs_s = [], []

    for l in range(depth):
        lam0 = _lambda_init(l)
        w_in_l = _permute_w_in(w_in[l], src, width, order, n_total)
        w_out_l = w_out[l].astype(BF16)
        mod_p = [mods[l, 0:bp, j * d:(j + 1) * d] for j in range(6)]
        mod_s = [jnp.repeat(mods[l, bp:bp + bs, j * d:(j + 1) * d], ls, axis=0) for j in range(6)]
        if l % 2 == 0:
            ffn_w = (ffn_w1[l // 2].astype(BF16), ffn_w3[l // 2].astype(BF16), ffn_w2[l // 2].astype(BF16))
        else:
            ffn_w = (moe_router[l // 2], moe_w1, moe_w3, moe_w2, l // 2)

        def run(x, nb, ln, mod, cos, sin, caches, ssm0, conv_b0, conv_c0):
            sh1, sc1, g1, sh2, sc2, g2 = mod
            h = _normmod(x, norm_mix[l], sc1, sh1, BF16)
            proj = _proj(h, w_in_l, cos, sin, tn_proj, n_rope_blocks)
            proj3 = proj.reshape(nb, ln, n_total)
            if caches is None:
                k_b = proj[:, cols["k"]:cols["k"] + d_attn].astype(BF16)
                v_t = proj[:, cols["v"]:cols["v"] + d_attn].T.astype(BF16)
                ya = _flash_attention(proj, k_b, v_t, cols["q"], n_heads_a, attn_lambda[l], attn_subln[l], lam0)
            else:
                ya = _cached_attention(proj3, cols["q"], cols["k"], cols["v"], n_heads_a, caches[0], caches[1],
                                       attn_lambda[l], attn_subln[l], lam0).reshape(nb * ln, d_attn)
            yb, ssm_new, conv_b_new = _ssd(proj3, cols, ssd_dims, conv_b0, ssd_conv_w[l], ssd_conv_b[l],
                                           ssd_dt_bias[l], ssd_a_log[l], ssd_d[l], ssd_norm[l], ssm0)
            yc, conv_c_new = _conf(proj3, cols["ga"], cols["gg"], d_conv, conv_c0, conf_dw_w[l], conf_dw_b[l],
                                   conf_ln_g[l], conf_ln_b[l])
            x = _outproj(ya, yb.reshape(nb * ln, d_ssd), yc.reshape(nb * ln, d_conv), w_out_l, x, g1)
            if l % 2 == 0:
                h2 = _normmod(x, norm_ffn[l], sc2, sh2, BF16)
                x = _ffn(h2, *ffn_w, x, g2)
            else:
                x = _moe(x, norm_ffn[l], sc2, sh2, g2, *ffn_w)
            k_new = proj3[:, :, cols["k"]:cols["k"] + d_attn].reshape(nb, ln, n_heads_a, 2, head_dim)
            v_new = proj3[:, :, cols["v"]:cols["v"] + d_attn].reshape(nb, ln, n_heads_a, 2 * head_dim)
            return x, (k_new, v_new, ssm_new, conv_b_new, conv_c_new)

        xp, st_p = run(xp, bp, lp, mod_p, cos_p, sin_p, None, zeros_ssm, zeros_cb, zeros_cc)
        caches = (cache_k[l].reshape(bs, past, d_attn), cache_v[l].reshape(bs, past, d_attn))
        xs, st_s = run(xs, bs, ls, mod_s, cos_s, sin_s, caches, state_ssm[l], state_conv_ssd[l], state_conv_conf[l])
        outs_p.append(st_p)
        outs_s.append(st_s)

    y_prompt = _normmod(xp, norm_final, None, None, F32).reshape(bp, lp, d)
    y_sample = _normmod(xs, norm_final, None, None, F32).reshape(bs, ls, d)
    stack = lambda outs, k: jnp.stack([o[k] for o in outs])
    return (y_prompt, y_sample,
            stack(outs_p, 0), stack(outs_p, 1), stack(outs_p, 2), stack(outs_p, 3), stack(outs_p, 4),
            stack(outs_s, 0), stack(outs_s, 1), stack(outs_s, 2), stack(outs_s, 3), stack(outs_s, 4))
```

```python
import functools
import math

import numpy as np
import jax
import jax.numpy as jnp
from jax import lax
from jax.experimental import pallas as pl
from jax.experimental.pallas import tpu as pltpu

F32 = jnp.float32
BF16 = jnp.bfloat16
U32 = jnp.uint32
I32 = jnp.int32

CHUNK = 64
ROPE_THETA = 10000.0
EPS = 1e-6

LANES = 128
SUBLANES = 8
VMEM_LIMIT_BYTES = 48 * 1024 * 1024

ROW_TILE = 512
ATTN_TQ = 1024
ATTN_TK = 1024
ATTN_GROUPS = 1
ONES_ROWS = 16
LOG2_E = 1.4426950408889634
SSD_CHUNK = 128
CONF_TILE = 256
EXPERT_TILE = 512
GATHER_TILE = 256


def _cparams(*sem):
    return pltpu.CompilerParams(dimension_semantics=sem, vmem_limit_bytes=VMEM_LIMIT_BYTES)


def _tile(n, pref):
    t = min(n, pref)
    assert n % t == 0, (n, pref)
    return t


def _rows_spec(arr, tm, width, by_col=False):
    if arr.shape[0] == 1:
        return pl.BlockSpec((1, width), lambda j, i: (0, j if by_col else 0))
    return pl.BlockSpec((tm, width), lambda j, i: (i, j if by_col else 0))


def _silu(x):
    return x * jax.nn.sigmoid(x)


def _precision(a, b):
    assert a.dtype == b.dtype, (a.dtype, b.dtype)
    return lax.Precision.HIGHEST if a.dtype == F32 else None


def _dot(a, b):
    return jnp.dot(a, b, preferred_element_type=F32, precision=_precision(a, b))


def _dot_nt(a, b):
    return lax.dot_general(a, b, (((1,), (1,)), ((), ())), preferred_element_type=F32, precision=_precision(a, b))


def _adaln_kernel(c_ref, w_ref, b_ref, o_ref):
    o_ref[0] = _dot(_silu(c_ref[...]), w_ref[0]) + b_ref[0]


def _adaln(c_all, w_ada, b_ada):
    depth, d, mc = w_ada.shape
    rows = c_all.shape[0]
    tn = _tile(mc, 1024)
    return pl.pallas_call(
        _adaln_kernel,
        out_shape=jax.ShapeDtypeStruct((depth, rows, mc), F32),
        grid=(depth, mc // tn),
        in_specs=[pl.BlockSpec((rows, d), lambda l, j: (0, 0)),
                  pl.BlockSpec((1, d, tn), lambda l, j: (l, 0, j)),
                  pl.BlockSpec((1, 1, tn), lambda l, j: (l, 0, j))],
        out_specs=pl.BlockSpec((1, rows, tn), lambda l, j: (l, 0, j)),
        compiler_params=_cparams("parallel", "parallel"),
        name="adaln",
    )(c_all, w_ada, b_ada.reshape(depth, 1, mc))


def _modulated(x, g, sc, sh):
    ms = jnp.mean(x * x, axis=-1, keepdims=True)
    y = x * lax.rsqrt(ms + EPS) * g
    if sc is not None:
        y = y * (1.0 + sc) + sh
    return y


def _normmod_kernel(x_ref, g_ref, sc_ref, sh_ref, o_ref):
    o_ref[...] = _modulated(x_ref[...], g_ref[...], sc_ref[...], sh_ref[...]).astype(o_ref.dtype)


def _norm_kernel(x_ref, g_ref, o_ref):
    o_ref[...] = _modulated(x_ref[...], g_ref[...], None, None).astype(o_ref.dtype)


def _normmod(x, g, sc, sh, out_dtype):
    t, d = x.shape
    tm = _tile(t, ROW_TILE)
    g = g.reshape(1, d)
    row = pl.BlockSpec((tm, d), lambda j, i: (i, 0))
    vec = pl.BlockSpec((1, d), lambda j, i: (0, 0))
    if sc is None:
        body, ops, specs = _norm_kernel, (x, g), [row, vec]
    else:
        body, ops = _normmod_kernel, (x, g, sc, sh)
        specs = [row, vec, _rows_spec(sc, tm, d), _rows_spec(sh, tm, d)]
    return pl.pallas_call(
        body, out_shape=jax.ShapeDtypeStruct((t, d), out_dtype),
        grid=(1, t // tm), in_specs=specs, out_specs=row,
        compiler_params=_cparams("parallel", "parallel"), name="normmod",
    )(*ops)


def _proj_kernel(a_ref, w_ref, cos_ref, sin_ref, o_ref, *extra_refs, n_rope_blocks, jk, jv):
    acc = _dot(a_ref[...], w_ref[...])
    j = pl.program_id(1)

    @pl.when(j >= n_rope_blocks)
    def _():
        o_ref[...] = acc

    @pl.when(j < n_rope_blocks)
    def _():
        cos = cos_ref[...]
        sin = sin_ref[...]
        lane = lax.broadcasted_iota(I32, cos.shape, 1)
        first_half = (lane % CHUNK) < (CHUNK // 2)
        for c in range(acc.shape[1] // LANES):
            blk = acc[:, c * LANES:(c + 1) * LANES]
            partner = jnp.where(first_half, pltpu.roll(blk, LANES - 32, 1), pltpu.roll(blk, 32, 1))
            o_ref[:, c * LANES:(c + 1) * LANES] = blk * cos + partner * sin

    if extra_refs:
        kb_ref, kt_ref, vt_ref = extra_refs

        @pl.when(j == jk)
        def _():
            k = o_ref[...]
            kb_ref[...] = k.astype(BF16)
            kt_ref[...] = k.T

        @pl.when(j == jv)
        def _():
            vt_ref[...] = acc.T.astype(BF16)


def _proj(h, w, cos, sin, tn, n_rope_blocks, extras=None):
    t, d = h.shape
    n = w.shape[1]
    tm = _tile(t, ROW_TILE)
    out_shape = [jax.ShapeDtypeStruct((t, n), F32)]
    out_specs = [pl.BlockSpec((tm, tn), lambda i, j: (i, j))]
    jk = jv = -1
    if extras is not None:
        assert extras[0] % tn == 0 and extras[1] % tn == 0
        jk, jv = extras[0] // tn, extras[1] // tn
        out_shape += [jax.ShapeDtypeStruct((t, tn), BF16), jax.ShapeDtypeStruct((tn, t), F32),
                      jax.ShapeDtypeStruct((tn, t), BF16)]
        out_specs += [pl.BlockSpec((tm, tn), lambda i, j: (i, 0)), pl.BlockSpec((tn, tm), lambda i, j: (0, i)),
                      pl.BlockSpec((tn, tm), lambda i, j: (0, i))]
    outs = pl.pallas_call(
        functools.partial(_proj_kernel, n_rope_blocks=n_rope_blocks, jk=jk, jv=jv),
        out_shape=out_shape,
        grid=(t // tm, n // tn),
        in_specs=[pl.BlockSpec((tm, d), lambda i, j: (i, 0)),
                  pl.BlockSpec((d, tn), lambda i, j: (0, j)),
                  pl.BlockSpec((tm, LANES), lambda i, j: (i, 0)),
                  pl.BlockSpec((tm, LANES), lambda i, j: (i, 0))],
        out_specs=out_specs,
        compiler_params=_cparams("parallel", "arbitrary"), name="proj",
    )(h, w, cos, sin)
    return outs if extras is not None else outs[0]


def _lambda_value(lam_ref, lambda_init):
    lp = lam_ref[...]
    a = jnp.sum(lp[0:1] * lp[1:2], axis=-1, keepdims=True)
    b = jnp.sum(lp[2:3] * lp[3:4], axis=-1, keepdims=True)
    return jnp.exp(a) - jnp.exp(b) + lambda_init


def _stack_maps(q):
    lane = lax.broadcasted_iota(I32, q.shape, 1)
    lo = lane < (LANES // 2)
    return jnp.concatenate([jnp.where(lo, q, 0.0), jnp.where(lo, 0.0, q)], axis=0)


def _diff_finish(o1, o2, lam, g, lambda_init):
    o = o1 - lam * o2
    ms = jnp.mean(o * o, axis=-1, keepdims=True)
    return o * lax.rsqrt(ms + EPS) * g * (1.0 - lambda_init)


def _flash_kernel(qi_ref, ki_ref, q_ref, k_ref, vt_ref, lam_ref, g_ref, o_ref,
                  qst_ref, m_ref, acc_ref, *, tq, tk, lambda_init):
    p = pl.program_id(1)
    qi = qi_ref[p]
    ki = ki_ref[p]

    @pl.when(ki == 0)
    def _():
        qt = (q_ref[...] * (float(CHUNK) ** -0.5 * LOG2_E)).T
        lo = lax.broadcasted_iota(I32, qt.shape, 0) < (LANES // 2)
        qst_ref[:, 0:tq] = jnp.where(lo, qt, 0.0).astype(BF16)
        qst_ref[:, tq:2 * tq] = jnp.where(lo, 0.0, qt).astype(BF16)
        m_ref[...] = jnp.full(m_ref.shape, -jnp.inf, F32)
        acc_ref[...] = jnp.zeros(acc_ref.shape, F32)

    def step(masked):
        vt_ones = jnp.concatenate([vt_ref[...], jnp.ones((ONES_ROWS, tk), BF16)], axis=0)
        cw = 2 * tq // ATTN_GROUPS
        for c in range(ATTN_GROUPS):
            cs = slice(c * cw, (c + 1) * cw)
            st = _dot(k_ref[...], qst_ref[:, cs])
            if masked:
                k_chunk = (ki * tk + lax.broadcasted_iota(I32, st.shape, 0)) // CHUNK
                q_chunk = (qi * tq + (c * cw + lax.broadcasted_iota(I32, st.shape, 1)) % tq) // CHUNK
                st = jnp.where(k_chunk <= q_chunk, st, -jnp.inf)
            m_prev = m_ref[:, cs]
            m_new = jnp.maximum(m_prev, jnp.max(st, axis=0, keepdims=True))
            alpha = jnp.exp2(m_prev - m_new)
            pr = jnp.exp2((st - m_new).astype(BF16))
            acc_ref[:, cs] = alpha * acc_ref[:, cs] + _dot(vt_ones, pr)
            m_ref[:, cs] = m_new

    needs_mask = (ki + 1) * tk > qi * tq

    @pl.when(jnp.logical_not(needs_mask))
    def _():
        step(False)

    @pl.when(needs_mask)
    def _():
        step(True)

    @pl.when((ki + 1) * tk == (qi + 1) * tq)
    def _():
        o = acc_ref[0:LANES, :] * (1.0 / acc_ref[LANES:LANES + 1, :])
        lam = _lambda_value(lam_ref, lambda_init)
        o = (o[:, 0:tq] - lam * o[:, tq:2 * tq]).T
        ms = jnp.mean(o * o, axis=-1, keepdims=True)
        o_ref[...] = (o * lax.rsqrt(ms + EPS) * g_ref[...] * (1.0 - lambda_init)).astype(o_ref.dtype)


def _flash_attention(proj, k_b, v_t, col_q, n_heads, lam_p, subln, lambda_init):
    t = proj.shape[0]
    tq = _tile(t, ATTN_TQ)
    tk = _tile(tq, ATTN_TK)
    assert tk % CHUNK == 0
    per_q = tq // tk
    pairs = [(a, b) for a in range(t // tq) for b in range((a + 1) * per_q)]
    qi = jnp.asarray(np.array([a for a, _ in pairs], np.int32))
    ki = jnp.asarray(np.array([b for _, b in pairs], np.int32))
    cq = col_q // LANES
    grid_spec = pltpu.PrefetchScalarGridSpec(
        num_scalar_prefetch=2,
        grid=(n_heads, len(pairs)),
        in_specs=[pl.BlockSpec((tq, LANES), lambda h, p, qi, ki: (qi[p], cq + h)),
                  pl.BlockSpec((tk, LANES), lambda h, p, qi, ki: (ki[p], h)),
                  pl.BlockSpec((LANES, tk), lambda h, p, qi, ki: (h, ki[p])),
                  pl.BlockSpec(lam_p.shape, lambda h, p, qi, ki: (0, 0)),
                  pl.BlockSpec((1, LANES), lambda h, p, qi, ki: (0, 0))],
        out_specs=pl.BlockSpec((tq, LANES), lambda h, p, qi, ki: (qi[p], h)),
        scratch_shapes=[pltpu.VMEM((LANES, 2 * tq), BF16),
                        pltpu.VMEM((1, 2 * tq), F32),
                        pltpu.VMEM((LANES + ONES_ROWS, 2 * tq), F32)])
    return pl.pallas_call(
        functools.partial(_flash_kernel, tq=tq, tk=tk, lambda_init=lambda_init),
        out_shape=jax.ShapeDtypeStruct((t, n_heads * LANES), BF16),
        grid_spec=grid_spec,
        compiler_params=_cparams("parallel", "arbitrary"), name="flash_attn",
    )(qi, ki, proj, k_b, v_t, lam_p, subln.reshape(1, LANES))


def _cached_attn_kernel(q_ref, kn_ref, vn_ref, kct_ref, vc_ref, lam_ref, g_ref, o_ref, *, lambda_init):
    ln = q_ref.shape[0]
    past = kct_ref.shape[1]
    qs = _stack_maps(q_ref[...] * (float(CHUNK) ** -0.5))
    sc = _dot(qs, kct_ref[...])
    sn = _dot_nt(qs, kn_ref[...])
    q_chunk_c = (past + lax.broadcasted_iota(I32, sc.shape, 0) % ln) // CHUNK
    q_chunk_n = (past + lax.broadcasted_iota(I32, sn.shape, 0) % ln) // CHUNK
    sc = jnp.where(lax.broadcasted_iota(I32, sc.shape, 1) // CHUNK <= q_chunk_c, sc, -jnp.inf)
    sn = jnp.where((past + lax.broadcasted_iota(I32, sn.shape, 1)) // CHUNK <= q_chunk_n, sn, -jnp.inf)
    m = jnp.maximum(jnp.max(sc, axis=1, keepdims=True), jnp.max(sn, axis=1, keepdims=True))
    pc = jnp.exp(sc - m)
    pn = jnp.exp(sn - m)
    denom = jnp.sum(pc, axis=1, keepdims=True) + jnp.sum(pn, axis=1, keepdims=True)
    o = (_dot(pc, vc_ref[...]) + _dot(pn, vn_ref[...])) / denom
    lam = _lambda_value(lam_ref, lambda_init)
    o_ref[...] = _diff_finish(o[0:ln], o[ln:2 * ln], lam, g_ref[...], lambda_init).astype(o_ref.dtype)


def _cached_attention(proj3, col_q, col_k, col_v, n_heads, k_cache_t, v_cache, li, lam_p, subln, lambda_init):
    b, ln, _ = proj3.shape
    past = v_cache.shape[2]
    cq, ck, cv = col_q // LANES, col_k // LANES, col_v // LANES
    new = lambda c: pl.BlockSpec((None, ln, LANES), lambda bi, h: (bi, 0, c + h))
    return pl.pallas_call(
        functools.partial(_cached_attn_kernel, lambda_init=lambda_init),
        out_shape=jax.ShapeDtypeStruct((b, ln, n_heads * LANES), F32),
        grid=(b, n_heads),
        in_specs=[new(cq), new(ck), new(cv),
                  pl.BlockSpec((None, None, LANES, past), lambda bi, h: (li, bi, h, 0)),
                  pl.BlockSpec((None, None, past, LANES), lambda bi, h: (li, bi, 0, h)),
                  pl.BlockSpec(lam_p.shape, lambda bi, h: (0, 0)),
                  pl.BlockSpec((1, LANES), lambda bi, h: (0, 0))],
        out_specs=pl.BlockSpec((None, ln, LANES), lambda bi, h: (bi, 0, h)),
        compiler_params=_cparams("parallel", "parallel"), name="cached_attn",
    )(proj3, proj3, proj3, k_cache_t, v_cache, lam_p, subln.reshape(1, LANES))


def _split3(x):
    x1 = x.astype(BF16)
    r1 = x - x1.astype(F32)
    x2 = r1.astype(BF16)
    x3 = (r1 - x2.astype(F32)).astype(BF16)
    return x1, x2, x3


def _ssd_kernel(xs_ref, bm_ref, cm_ref, hx_ref, hb_ref, hc_ref, sx_ref, sb_ref, sc_ref,
                wx_ref, wb_ref, wc_ref, bx_ref, bb_ref, bc_ref,
                dt_ref, dtb_ref, alog_ref, z_ref, dexp_ref, gn_ref, h0_ref,
                y_ref, hout_ref, nsx_ref, nsb_ref, nsc_ref,
                scr_x, scr_b, scr_c, h_scr, y_scr, xte_scr,
                *, q, qp, n_heads, p_dim, n_state, n_groups, width):
    i = pl.program_id(1)
    first = i == 0
    halo = SUBLANES

    @pl.when(first)
    def _():
        h_scr[...] = h0_ref[...]

    def conv(t_ref, halo_ref, st_ref, w_ref, b_ref, scr, ns_ref):
        scr[0:halo] = jnp.where(first, st_ref[...], halo_ref[...])
        scr[halo:halo + q] = t_ref[...]
        acc = b_ref[...] + w_ref[0:1, :] * scr[halo - width + 1:halo - width + 1 + q]
        for k in range(1, width):
            acc = acc + w_ref[k:k + 1, :] * scr[halo - width + 1 + k:halo - width + 1 + k + q]
        ns_ref[...] = scr[q:q + halo]
        return _silu(acc)

    xs = conv(xs_ref, hx_ref, sx_ref, wx_ref, bx_ref, scr_x, nsx_ref)
    bm = conv(bm_ref, hb_ref, sb_ref, wb_ref, bb_ref, scr_b, nsb_ref)
    cm = conv(cm_ref, hc_ref, sc_ref, wc_ref, bc_ref, scr_c, nsc_ref)
    dt = jax.nn.softplus(dt_ref[...] + dtb_ref[...])
    a = -jnp.exp(alog_ref[...])

    def pad(v):
        if qp == q:
            return v
        return jnp.concatenate([v, jnp.zeros((qp - q, v.shape[1]), v.dtype)], axis=0)

    xs_p, bm_p, cm_p, dt_p = pad(xs), pad(bm), pad(cm), pad(dt)
    d_a = dt_p * a
    row = lax.broadcasted_iota(I32, (qp, qp), 0)
    col = lax.broadcasted_iota(I32, (qp, qp), 1)
    tril = row >= col
    ones_tril = jnp.where(tril, 1.0, 0.0).astype(BF16)
    d1, d2, d3 = _split3(d_a)
    acum = _dot(ones_tril, d1) + _dot(ones_tril, d2) + _dot(ones_tril, d3)
    acum_t = acum.T
    total = acum[qp - 1:qp, :]
    mm = y_ref.dtype
    bm_b = bm_p.astype(mm)
    cm_b = cm_p.astype(mm)
    scores = [_dot_nt(cm_b[:, g * n_state:(g + 1) * n_state], bm_b[:, g * n_state:(g + 1) * n_state])
              for g in range(n_groups)]
    rep = n_heads // n_groups
    for h in range(n_heads):
        g = h // rep
        colv = acum[:, h:h + 1]
        rowv = acum_t[h:h + 1, :]
        tot = total[:, h:h + 1]
        decay = jnp.where(tril, jnp.exp(colv - rowv), 0.0)
        xdt = xs_p[:, h * p_dim:(h + 1) * p_dim] * dt_p[:, h:h + 1]
        y_diag = _dot((scores[g] * decay).astype(mm), xdt.astype(mm))
        h_prev = h_scr[h]
        y_off = jnp.exp(colv) * _dot_nt(cm_b[:, g * n_state:(g + 1) * n_state], h_prev.astype(mm))
        y_scr[:, h * p_dim:(h + 1) * p_dim] = y_diag + y_off
        xte_scr[:, h * p_dim:(h + 1) * p_dim] = xdt * jnp.exp(tot - colv)
    xte_t = xte_scr[...].T.astype(mm)
    for h in range(n_heads):
        g = h // rep
        st = _dot(xte_t[h * p_dim:(h + 1) * p_dim, :], bm_b[:, g * n_state:(g + 1) * n_state])
        h_scr[h] = jnp.exp(total[:, h:h + 1]) * h_scr[h] + st
    y = y_scr[0:q] + dexp_ref[...] * xs
    yg = y * _silu(z_ref[...])
    ms = jnp.mean(yg * yg, axis=-1, keepdims=True)
    y_ref[...] = (yg * lax.rsqrt(ms + EPS) * gn_ref[...]).astype(y_ref.dtype)
    hout_ref[...] = h_scr[...]


def _ssd(proj3, cols, dims, conv_state, conv_w, conv_b, dt_bias, a_log, d_skip, norm_g, h0, out_dtype):
    b, ln, _ = proj3.shape
    n_heads, p_dim, n_state, n_groups, width = dims
    d_ssd = n_heads * p_dim
    gn = n_groups * n_state
    q = _tile(ln, SSD_CHUNK)
    qp = max(q, LANES)
    assert q % SUBLANES == 0 and ln >= width - 1 and width - 1 <= SUBLANES and n_heads <= LANES
    nsteps = ln // q
    halo = SUBLANES
    widths = (d_ssd, gn, gn)
    offs = (cols["xs"], cols["bm"], cols["cm"])
    for o, w in zip(offs, widths):
        assert o % w == 0
    assert cols["z"] % d_ssd == 0 and cols["dt"] % LANES == 0

    def tile_spec(o, w):
        return pl.BlockSpec((None, q, w), lambda bi, i: (bi, i, o // w))

    def halo_spec(o, w):
        return pl.BlockSpec((None, halo, w), lambda bi, i: (bi, jnp.maximum(i * (q // halo) - 1, 0), o // w))

    def state_spec(w):
        return pl.BlockSpec((None, halo, w), lambda bi, i: (bi, 0, 0))

    def const_spec(shape):
        return pl.BlockSpec(shape, lambda bi, i: (0,) * len(shape))

    st = jnp.pad(conv_state, ((0, 0), (halo - (width - 1), 0), (0, 0)))
    bounds = (0, d_ssd, d_ssd + gn, d_ssd + 2 * gn)
    st_segs = [st[..., bounds[k]:bounds[k + 1]] for k in range(3)]
    w_segs = [conv_w[:, bounds[k]:bounds[k + 1]] for k in range(3)]
    b_segs = [conv_b[bounds[k]:bounds[k + 1]].reshape(1, -1) for k in range(3)]
    pad_lane = lambda v: jnp.pad(v.reshape(1, -1), ((0, 0), (0, LANES - v.shape[0])))
    in_specs = ([tile_spec(o, w) for o, w in zip(offs, widths)]
                + [halo_spec(o, w) for o, w in zip(offs, widths)]
                + [state_spec(w) for w in widths]
                + [const_spec((width, w)) for w in widths]
                + [const_spec((1, w)) for w in widths]
                + [tile_spec(cols["dt"], LANES), const_spec((1, LANES)), const_spec((1, LANES)),
                   tile_spec(cols["z"], d_ssd), const_spec((1, d_ssd)), const_spec((1, d_ssd)),
                   pl.BlockSpec((None, n_heads, p_dim, n_state), lambda bi, i: (bi, 0, 0, 0))])
    out_shape = (jax.ShapeDtypeStruct((b, ln, d_ssd), out_dtype),
                 jax.ShapeDtypeStruct((b, n_heads, p_dim, n_state), F32),
                 jax.ShapeDtypeStruct((b, halo, d_ssd), F32),
                 jax.ShapeDtypeStruct((b, halo, gn), F32),
                 jax.ShapeDtypeStruct((b, halo, gn), F32))
    out_specs = (pl.BlockSpec((None, q, d_ssd), lambda bi, i: (bi, i, 0)),
                 pl.BlockSpec((None, n_heads, p_dim, n_state), lambda bi, i: (bi, 0, 0, 0)),
                 state_spec(d_ssd), state_spec(gn), state_spec(gn))
    scratch = [pltpu.VMEM((q + halo, d_ssd), F32), pltpu.VMEM((q + halo, gn), F32), pltpu.VMEM((q + halo, gn), F32),
               pltpu.VMEM((n_heads, p_dim, n_state), F32),
               pltpu.VMEM((qp, d_ssd), F32), pltpu.VMEM((qp, d_ssd), F32)]
    y, h_new, nsx, nsb, nsc = pl.pallas_call(
        functools.partial(_ssd_kernel, q=q, qp=qp, n_heads=n_heads, p_dim=p_dim, n_state=n_state,
                          n_groups=n_groups, width=width),
        out_shape=out_shape, grid=(b, nsteps), in_specs=in_specs, out_specs=out_specs,
        scratch_shapes=scratch,
        compiler_params=_cparams("parallel", "arbitrary"), name="ssd",
    )(proj3, proj3, proj3, proj3, proj3, proj3, *st_segs, *w_segs, *b_segs,
      proj3, pad_lane(dt_bias), pad_lane(a_log), proj3,
      jnp.repeat(d_skip, p_dim).reshape(1, d_ssd), norm_g.reshape(1, d_ssd), h0)
    new_state = jnp.concatenate([nsx, nsb, nsc], axis=-1)[:, halo - (width - 1):, :]
    return y, h_new, new_state


def _conf_kernel(ga_ref, gg_ref, hga_ref, hgg_ref, st_ref, w_ref, b_ref, lg_ref, lb_ref,
                 y_ref, ns_ref, scr, *, tb, width, halo):
    i = pl.program_id(1)
    u = ga_ref[...] * jax.nn.sigmoid(gg_ref[...])
    uh = hga_ref[...] * jax.nn.sigmoid(hgg_ref[...])
    scr[0:halo] = jnp.where(i == 0, st_ref[...], uh)
    scr[halo:halo + tb] = u
    base = halo - (width - 1)
    acc = b_ref[...] + w_ref[0:1, :] * scr[base:base + tb]
    for k in range(1, width):
        acc = acc + w_ref[k:k + 1, :] * scr[base + k:base + k + tb]
    mu = jnp.mean(acc, axis=-1, keepdims=True)
    cen = acc - mu
    var = jnp.mean(cen * cen, axis=-1, keepdims=True)
    y = cen * lax.rsqrt(var + EPS) * lg_ref[...] + lb_ref[...]
    y_ref[...] = _silu(y).astype(y_ref.dtype)
    ns_ref[...] = scr[tb:tb + halo]


def _conf(proj3, col_ga, col_gg, d_conv, conv_state, w, bias, ln_g, ln_b, out_dtype):
    b, ln, _ = proj3.shape
    width = w.shape[0]
    halo = -(-(width - 1) // SUBLANES) * SUBLANES
    tb = _tile(ln, CONF_TILE)
    assert tb % halo == 0 and ln >= width - 1 and col_ga % d_conv == 0 and col_gg % d_conv == 0
    st = jnp.pad(conv_state, ((0, 0), (halo - (width - 1), 0), (0, 0)))
    tile_spec = lambda o: pl.BlockSpec((None, tb, d_conv), lambda bi, i: (bi, i, o // d_conv))
    halo_spec = lambda o: pl.BlockSpec((None, halo, d_conv),
                                       lambda bi, i: (bi, jnp.maximum(i * (tb // halo) - 1, 0), o // d_conv))
    st_spec = pl.BlockSpec((None, halo, d_conv), lambda bi, i: (bi, 0, 0))
    vec = pl.BlockSpec((1, d_conv), lambda bi, i: (0, 0))
    y, ns = pl.pallas_call(
        functools.partial(_conf_kernel, tb=tb, width=width, halo=halo),
        out_shape=(jax.ShapeDtypeStruct((b, ln, d_conv), out_dtype), jax.ShapeDtypeStruct((b, halo, d_conv), F32)),
        grid=(b, ln // tb),
        in_specs=[tile_spec(col_ga), tile_spec(col_gg), halo_spec(col_ga), halo_spec(col_gg), st_spec,
                  pl.BlockSpec((width, d_conv), lambda bi, i: (0, 0)), vec, vec, vec],
        out_specs=(pl.BlockSpec((None, tb, d_conv), lambda bi, i: (bi, i, 0)), st_spec),
        scratch_shapes=[pltpu.VMEM((tb + halo, d_conv), F32)],
        compiler_params=_cparams("parallel", "arbitrary"), name="conf_conv",
    )(proj3, proj3, proj3, proj3, st, w, bias.reshape(1, -1), ln_g.reshape(1, -1), ln_b.reshape(1, -1))
    return y, ns[:, halo - (width - 1):, :]


def _outproj_kernel(ya_ref, yb_ref, yc_ref, wa_ref, wb_ref, wc_ref, res_ref, gate_ref, o_ref):
    acc = _dot(ya_ref[...], wa_ref[...]) + _dot(yb_ref[...], wb_ref[...]) + _dot(yc_ref[...], wc_ref[...])
    o_ref[...] = res_ref[...] + gate_ref[...] * acc


def _outproj(ya, yb, yc, w_out, res, gate):
    t, d = res.shape
    da, db = ya.shape[1], yb.shape[1]
    tm = _tile(t, ROW_TILE)
    tn = _tile(d, 1024)
    a_spec = lambda k: pl.BlockSpec((tm, k), lambda j, i: (i, 0))
    w_spec = lambda k: pl.BlockSpec((k, tn), lambda j, i: (0, j))
    return pl.pallas_call(
        _outproj_kernel, out_shape=jax.ShapeDtypeStruct((t, d), F32),
        grid=(d // tn, t // tm),
        in_specs=[a_spec(da), a_spec(db), a_spec(yc.shape[1]), w_spec(da), w_spec(db), w_spec(yc.shape[1]),
                  pl.BlockSpec((tm, tn), lambda j, i: (i, j)), _rows_spec(gate, tm, tn, by_col=True)],
        out_specs=pl.BlockSpec((tm, tn), lambda j, i: (i, j)),
        compiler_params=_cparams("parallel", "parallel"), name="outproj",
    )(ya, yb, yc, w_out[:da], w_out[da:da + db], w_out[da + db:], res, gate)


def _ffn_up_kernel(a_ref, w1_ref, w3_ref, o_ref):
    a = a_ref[...]
    o_ref[...] = (_silu(_dot(a, w1_ref[...])) * _dot(a, w3_ref[...])).astype(o_ref.dtype)


def _ffn_down_kernel(a_ref, w_ref, res_ref, gate_ref, o_ref):
    o_ref[...] = res_ref[...] + gate_ref[...] * _dot(a_ref[...], w_ref[...])


def _ffn(h, w1, w3, w2, res, gate):
    t, d = h.shape
    f = w1.shape[1]
    tm = _tile(t, ROW_TILE)
    tn = _tile(f, 512)
    up = pl.pallas_call(
        _ffn_up_kernel, out_shape=jax.ShapeDtypeStruct((t, f), h.dtype),
        grid=(f // tn, t // tm),
        in_specs=[pl.BlockSpec((tm, d), lambda j, i: (i, 0)),
                  pl.BlockSpec((d, tn), lambda j, i: (0, j)),
                  pl.BlockSpec((d, tn), lambda j, i: (0, j))],
        out_specs=pl.BlockSpec((tm, tn), lambda j, i: (i, j)),
        compiler_params=_cparams("parallel", "parallel"), name="ffn_up",
    )(h, w1, w3)
    tn2 = _tile(d, 512)
    return pl.pallas_call(
        _ffn_down_kernel, out_shape=jax.ShapeDtypeStruct((t, d), F32),
        grid=(d // tn2, t // tm),
        in_specs=[pl.BlockSpec((tm, f), lambda j, i: (i, 0)),
                  pl.BlockSpec((f, tn2), lambda j, i: (0, j)),
                  pl.BlockSpec((tm, tn2), lambda j, i: (i, j)),
                  _rows_spec(gate, tm, tn2, by_col=True)],
        out_specs=pl.BlockSpec((tm, tn2), lambda j, i: (i, j)),
        compiler_params=_cparams("parallel", "parallel"), name="ffn_down",
    )(up, w2, res, gate)


def _router_kernel(x_ref, g_ref, sc_ref, sh_ref, r_ref, hp_ref, sel_ref, gate_ref, tot_ref, *, n_experts):
    h = _modulated(x_ref[...], g_ref[...], sc_ref[...], sh_ref[...])
    if hp_ref.dtype == U32:
        half = h.shape[1] // 2
        lo = lax.bitcast_convert_type(h[:, :half].astype(BF16).astype(F32), U32)
        hi = lax.bitcast_convert_type(h[:, half:].astype(BF16).astype(F32), U32)
        hp_ref[...] = (lo >> 16) | hi
    else:
        hp_ref[...] = h
    r = r_ref[...]
    h1 = h.astype(BF16)
    h2 = (h - h1.astype(F32)).astype(BF16)
    r1 = r.astype(BF16)
    r2 = (r - r1.astype(F32)).astype(BF16)
    logits = _dot(h1, r1) + _dot(h1, r2) + _dot(h2, r1)
    lane = lax.broadcasted_iota(I32, logits.shape, 1)
    lane_f = lane.astype(F32)
    lg = jnp.where(lane < n_experts, logits, -jnp.inf)
    v1 = jnp.max(lg, axis=1, keepdims=True)
    i1 = jnp.min(jnp.where(lg == v1, lane_f, float(LANES)), axis=1, keepdims=True)
    lg2 = jnp.where(lane_f == i1, -jnp.inf, lg)
    v2 = jnp.max(lg2, axis=1, keepdims=True)
    i2 = jnp.min(jnp.where(lg2 == v2, lane_f, float(LANES)), axis=1, keepdims=True)
    e = jnp.exp(v2 - v1)
    g1 = 1.0 / (1.0 + e)
    g2 = e * g1
    first = lane_f == i1
    second = lane_f == i2
    sel_ref[...] = jnp.where(first, 1, jnp.where(second, 2, 0)).astype(I32)
    gate_ref[...] = jnp.where(lane == 0, g1, jnp.where(lane == 1, g2, 0.0))
    cnt = jnp.sum(jnp.where(first | second, 1.0, 0.0), axis=0, keepdims=True)

    @pl.when(pl.program_id(1) == 0)
    def _():
        tot_ref[...] = jnp.zeros(tot_ref.shape, F32)

    tot_ref[...] += cnt


def _router(x, g, sc, sh, router_w, packed):
    t, d = x.shape
    n_experts = router_w.shape[1]
    tm = _tile(t, ROW_TILE)
    r = jnp.pad(router_w, ((0, 0), (0, LANES - n_experts)))
    row = lambda w: pl.BlockSpec((tm, w), lambda j, i: (i, 0))
    hw, hdt = (d // 2, U32) if packed else (d, F32)
    return pl.pallas_call(
        functools.partial(_router_kernel, n_experts=n_experts),
        out_shape=(jax.ShapeDtypeStruct((t, hw), hdt), jax.ShapeDtypeStruct((t, LANES), I32),
                   jax.ShapeDtypeStruct((t, LANES), F32), jax.ShapeDtypeStruct((1, LANES), F32)),
        grid=(1, t // tm),
        in_specs=[row(d), pl.BlockSpec((1, d), lambda j, i: (0, 0)), _rows_spec(sc, tm, d), _rows_spec(sh, tm, d),
                  pl.BlockSpec((d, LANES), lambda j, i: (0, 0))],
        out_specs=(row(hw), row(LANES), row(LANES), pl.BlockSpec((1, LANES), lambda j, i: (0, 0))),
        compiler_params=_cparams("arbitrary", "arbitrary"), name="router",
    )(x, g.reshape(1, d), sc, sh, r)


def _route_kernel(sel_ref, off_ref, pos_ref, carry_ref):
    @pl.when(pl.program_id(0) == 0)
    def _():
        carry_ref[...] = jnp.zeros(carry_ref.shape, F32)

    s = sel_ref[...]
    tm = s.shape[0]
    cnt = jnp.where(s > 0, 1.0, 0.0)
    row = lax.broadcasted_iota(I32, (tm, tm), 0)
    col = lax.broadcasted_iota(I32, (tm, tm), 1)
    before = jnp.where(row > col, 1.0, 0.0).astype(BF16)
    dest = _dot(before, cnt.astype(BF16)) + carry_ref[...] + off_ref[...]
    p1 = jnp.sum(jnp.where(s == 1, dest, 0.0), axis=1, keepdims=True)
    p2 = jnp.sum(jnp.where(s == 2, dest, 0.0), axis=1, keepdims=True)
    lane = lax.broadcasted_iota(I32, s.shape, 1)
    pos_ref[...] = jnp.where(lane == 0, p1, jnp.where(lane == 1, p2, 0.0)).astype(I32)
    carry_ref[...] += jnp.sum(cnt, axis=0, keepdims=True)


def _route(sel, offsets):
    t = sel.shape[0]
    tm = _tile(t, ROW_TILE)
    return pl.pallas_call(
        _route_kernel, out_shape=jax.ShapeDtypeStruct((t, LANES), I32),
        grid=(t // tm,),
        in_specs=[pl.BlockSpec((tm, LANES), lambda i: (i, 0)), pl.BlockSpec((1, LANES), lambda i: (0, 0))],
        out_specs=pl.BlockSpec((tm, LANES), lambda i: (i, 0)),
        scratch_shapes=[pltpu.VMEM((1, LANES), F32)],
        compiler_params=_cparams("arbitrary"), name="route",
    )(sel, offsets)


def _dispatch_kernel(p1_ref, p2_ref, hp_ref, init_ref, xs_ref, sem, *, tr):
    del init_ref
    base = pl.program_id(0) * tr

    def copy(r, dst):
        return pltpu.make_async_copy(hp_ref.at[pl.ds(r, 1), :], xs_ref.at[pl.ds(dst, 1), :], sem)

    def issue(r, c):
        copy(r, p1_ref[base + r]).start()
        copy(r, p2_ref[base + r]).start()
        return c

    def drain(r, c):
        copy(r, p1_ref[base + r]).wait()
        copy(r, p2_ref[base + r]).wait()
        return c

    lax.fori_loop(0, tr, issue, 0)
    lax.fori_loop(0, tr, drain, 0)


def _dispatch(hp, pos1, pos2, n_rows):
    t, w = hp.shape
    tr = _tile(t, GATHER_TILE)
    grid_spec = pltpu.PrefetchScalarGridSpec(
        num_scalar_prefetch=2, grid=(t // tr,),
        in_specs=[pl.BlockSpec((tr, w), lambda i, p1, p2: (i, 0)), pl.BlockSpec(memory_space=pl.ANY)],
        out_specs=pl.BlockSpec(memory_space=pl.ANY),
        scratch_shapes=[pltpu.SemaphoreType.DMA(())])
    return pl.pallas_call(
        functools.partial(_dispatch_kernel, tr=tr),
        out_shape=jax.ShapeDtypeStruct((n_rows, w), hp.dtype), grid_spec=grid_spec,
        input_output_aliases={3: 0},
        compiler_params=_cparams("arbitrary"), name="dispatch",
    )(pos1, pos2, hp, jnp.zeros((n_rows, w), hp.dtype))


def _unpack(u):
    lo = lax.bitcast_convert_type(u << 16, F32).astype(BF16)
    hi = lax.bitcast_convert_type(u & jnp.uint32(0xFFFF0000), F32).astype(BF16)
    return lo, hi


def _new_expert(te_ref, i):
    return jnp.logical_or(i == 0, te_ref[i] != te_ref[jnp.maximum(i - 1, 0)])


def _moe_up_kernel(te_ref, nv_ref, xs_ref, w1_ref, w3_ref, o_ref, w1b_ref, w3b_ref):
    i = pl.program_id(1)
    live = i < nv_ref[0]

    @pl.when(jnp.logical_and(live, _new_expert(te_ref, i)))
    def _():
        w1b_ref[...] = w1_ref[...].astype(BF16)
        w3b_ref[...] = w3_ref[...].astype(BF16)

    @pl.when(live)
    def _():
        lo, hi = _unpack(xs_ref[...])
        half = lo.shape[1]
        a = _dot(lo, w1b_ref[0:half, :]) + _dot(hi, w1b_ref[half:2 * half, :])
        b = _dot(lo, w3b_ref[0:half, :]) + _dot(hi, w3b_ref[half:2 * half, :])
        o_ref[...] = (_silu(a) * b).astype(o_ref.dtype)

    @pl.when(jnp.logical_not(live))
    def _():
        o_ref[...] = jnp.zeros(o_ref.shape, o_ref.dtype)


def _moe_down_kernel(te_ref, nv_ref, a_ref, w_ref, o_ref, wb_ref):
    i = pl.program_id(1)
    live = i < nv_ref[0]

    @pl.when(jnp.logical_and(live, _new_expert(te_ref, i)))
    def _():
        wb_ref[...] = w_ref[...].astype(BF16)

    @pl.when(live)
    def _():
        o_ref[...] = _dot(a_ref[...], wb_ref[...])

    @pl.when(jnp.logical_not(live))
    def _():
        o_ref[...] = jnp.zeros(o_ref.shape, o_ref.dtype)


def _moe_up_f32_kernel(te_ref, nv_ref, xs_ref, w1_ref, w3_ref, o_ref):
    @pl.when(pl.program_id(1) < nv_ref[0])
    def _():
        x = xs_ref[...]
        o_ref[...] = _silu(_dot(x, w1_ref[...])) * _dot(x, w3_ref[...])

    @pl.when(pl.program_id(1) >= nv_ref[0])
    def _():
        o_ref[...] = jnp.zeros(o_ref.shape, o_ref.dtype)


def _moe_down_f32_kernel(te_ref, nv_ref, a_ref, w_ref, o_ref):
    @pl.when(pl.program_id(1) < nv_ref[0])
    def _():
        o_ref[...] = _dot(a_ref[...], w_ref[...])

    @pl.when(pl.program_id(1) >= nv_ref[0])
    def _():
        o_ref[...] = jnp.zeros(o_ref.shape, o_ref.dtype)


def _moe_experts(xs, tile_expert, n_valid, w1, w3, w2, tm, li):
    n_rows, xw = xs.shape
    _, _, d, f = w1.shape
    packed = xs.dtype == U32
    n_tiles = n_rows // tm
    tn = _tile(f, 512)
    live = lambda i, nv: jnp.minimum(i, nv[0] - 1)
    up = pl.pallas_call(
        _moe_up_kernel if packed else _moe_up_f32_kernel,
        out_shape=jax.ShapeDtypeStruct((n_rows, f), BF16 if packed else F32),
        grid_spec=pltpu.PrefetchScalarGridSpec(
            num_scalar_prefetch=2, grid=(f // tn, n_tiles),
            in_specs=[pl.BlockSpec((tm, xw), lambda j, i, te, nv: (live(i, nv), 0)),
                      pl.BlockSpec((None, None, d, tn), lambda j, i, te, nv: (li, te[live(i, nv)], 0, j)),
                      pl.BlockSpec((None, None, d, tn), lambda j, i, te, nv: (li, te[live(i, nv)], 0, j))],
            out_specs=pl.BlockSpec((tm, tn), lambda j, i, te, nv: (i, j)),
            scratch_shapes=[pltpu.VMEM((d, tn), BF16), pltpu.VMEM((d, tn), BF16)] if packed else []),
        compiler_params=_cparams("parallel", "arbitrary"), name="moe_up",
    )(tile_expert, n_valid, xs, w1, w3)
    split = 2 if tm % (4 * SUBLANES) == 0 and tm >= 256 else 1
    tm2 = tm // split
    te2 = jnp.repeat(tile_expert, split)
    nv2 = n_valid * split
    tn2 = _tile(d, 512)
    return pl.pallas_call(
        _moe_down_kernel if packed else _moe_down_f32_kernel,
        out_shape=jax.ShapeDtypeStruct((n_rows, d), F32),
        grid_spec=pltpu.PrefetchScalarGridSpec(
            num_scalar_prefetch=2, grid=(d // tn2, n_tiles * split),
            in_specs=[pl.BlockSpec((tm2, f), lambda j, i, te, nv: (live(i, nv), 0)),
                      pl.BlockSpec((None, None, f, tn2), lambda j, i, te, nv: (li, te[live(i, nv)], 0, j))],
            out_specs=pl.BlockSpec((tm2, tn2), lambda j, i, te, nv: (i, j)),
            scratch_shapes=[pltpu.VMEM((f, tn2), BF16)] if packed else []),
        compiler_params=_cparams("parallel", "arbitrary"), name="moe_down",
    )(te2, nv2, up, w2)


def _combine_kernel(p1_ref, p2_ref, x_ref, gates_ref, gmod_ref, y_ref, o_ref, buf1, buf2, sem, *, tr):
    base = pl.program_id(0) * tr

    def copy(src, buf, r):
        return pltpu.make_async_copy(y_ref.at[pl.ds(src, 1), :], buf.at[pl.ds(r, 1), :], sem)

    def issue(r, c):
        copy(p1_ref[base + r], buf1, r).start()
        copy(p2_ref[base + r], buf2, r).start()
        return c

    def drain(r, c):
        copy(p1_ref[base + r], buf1, r).wait()
        copy(p2_ref[base + r], buf2, r).wait()
        return c

    lax.fori_loop(0, tr, issue, 0)
    lax.fori_loop(0, tr, drain, 0)
    g = gates_ref[...]
    f = g[:, 0:1] * buf1[...] + g[:, 1:2] * buf2[...]
    o_ref[...] = x_ref[...] + gmod_ref[...] * f


def _combine(x, gates, gmod, y, pos1, pos2):
    t, d = x.shape
    tr = _tile(t, GATHER_TILE)
    gspec = (pl.BlockSpec((1, d), lambda i, p1, p2: (0, 0)) if gmod.shape[0] == 1
             else pl.BlockSpec((tr, d), lambda i, p1, p2: (i, 0)))
    grid_spec = pltpu.PrefetchScalarGridSpec(
        num_scalar_prefetch=2, grid=(t // tr,),
        in_specs=[pl.BlockSpec((tr, d), lambda i, p1, p2: (i, 0)),
                  pl.BlockSpec((tr, LANES), lambda i, p1, p2: (i, 0)),
                  gspec, pl.BlockSpec(memory_space=pl.ANY)],
        out_specs=pl.BlockSpec((tr, d), lambda i, p1, p2: (i, 0)),
        scratch_shapes=[pltpu.VMEM((tr, d), F32), pltpu.VMEM((tr, d), F32), pltpu.SemaphoreType.DMA(())])
    return pl.pallas_call(
        functools.partial(_combine_kernel, tr=tr),
        out_shape=jax.ShapeDtypeStruct((t, d), F32), grid_spec=grid_spec,
        compiler_params=_cparams("arbitrary"), name="combine",
    )(pos1, pos2, x, gates, gmod, y)


def _moe(x, g, sc, sh, gmod, router_w, w1, w3, w2, li, packed):
    t, d = x.shape
    n_experts = router_w.shape[1]
    tm = min(EXPERT_TILE, max(SUBLANES * 2, (2 * t) // n_experts))
    n_tiles = -(-2 * t // tm) + n_experts
    hp, sel, gates, totals = _router(x, g, sc, sh, router_w, packed)
    counts = totals[0, :n_experts].astype(I32)
    tiles_per = (counts + tm - 1) // tm
    tile_end = jnp.cumsum(tiles_per)
    offsets = jnp.pad(((tile_end - tiles_per) * tm).astype(F32), (0, LANES - n_experts)).reshape(1, LANES)
    tile_expert = jnp.minimum(jnp.sum(jnp.arange(n_tiles, dtype=I32)[:, None] >= tile_end[None, :], axis=1),
                              n_experts - 1).astype(I32)
    n_valid = tile_end[n_experts - 1:].astype(I32)
    pos = _route(sel, offsets)
    pos1, pos2 = pos[:, 0], pos[:, 1]
    xs = _dispatch(hp, pos1, pos2, n_tiles * tm)
    y = _moe_experts(xs, tile_expert, n_valid, w1, w3, w2, tm, li)
    return _combine(x, gates, gmod, y, pos1, pos2)


def _rope_tables(pos):
    half = CHUNK // 2
    inv = ROPE_THETA ** (-jnp.arange(half, dtype=F32) / half)
    ang = pos.astype(F32)[:, None] * inv[None, :]
    cos, sin = jnp.cos(ang), jnp.sin(ang)
    reps = LANES // CHUNK
    return jnp.tile(cos, (1, 2 * reps)), jnp.tile(jnp.concatenate([-sin, sin], axis=1), (1, reps))


def _lambda_init(layer):
    return 0.8 - 0.6 * math.exp(-0.3 * layer)


def _layout(d_attn, d_ssd, gn, n_heads_b, d_conv):
    src = {"q": 0, "k": d_attn, "v": 2 * d_attn, "z": 3 * d_attn, "xs": 3 * d_attn + d_ssd,
           "bm": 3 * d_attn + 2 * d_ssd, "cm": 3 * d_attn + 2 * d_ssd + gn,
           "dt": 3 * d_attn + 2 * d_ssd + 2 * gn, "ga": 3 * d_attn + 2 * d_ssd + 2 * gn + n_heads_b,
           "gg": 3 * d_attn + 2 * d_ssd + 2 * gn + n_heads_b + d_conv}
    width = {"q": d_attn, "k": d_attn, "v": d_attn, "z": d_ssd, "xs": d_ssd, "bm": gn, "cm": gn,
             "dt": n_heads_b, "ga": d_conv, "gg": d_conv}
    order = ["q", "k", "v", "xs", "bm", "cm", "z", "ga", "gg", "dt"]
    cols, off = {}, 0
    for name in order:
        cols[name] = off
        off += LANES if name == "dt" else width[name]
    tn = min(1024, d_attn)
    total = -(-off // tn) * tn
    return src, width, order, cols, total, tn


def _permute_w_in(w, src, width, order, total):
    parts = []
    for name in order:
        blk = w[:, src[name]:src[name] + width[name]]
        if name == "dt":
            blk = jnp.pad(blk, ((0, 0), (0, LANES - width[name])))
        parts.append(blk)
    out = jnp.concatenate(parts, axis=1)
    return jnp.pad(out, ((0, 0), (0, total - out.shape[1])))


def kernel(x_prompt, x_sample, c_prompt, c_sample, cache_k, cache_v, state_ssm, state_conv_ssd, state_conv_conf,
           w_ada, b_ada, norm_mix, norm_ffn, w_in, w_out, attn_lambda, attn_subln, ssd_conv_w, ssd_conv_b,
           ssd_dt_bias, ssd_a_log, ssd_d, ssd_norm, conf_dw_w, conf_dw_b, conf_ln_g, conf_ln_b,
           ffn_w1, ffn_w3, ffn_w2, moe_router, moe_w1, moe_w3, moe_w2, norm_final):
    bp, lp, d = x_prompt.shape
    bs, ls, _ = x_sample.shape
    depth = w_in.shape[0]
    past = cache_k.shape[2]
    n_heads_a, head_dim = cache_k.shape[3], cache_k.shape[5]
    assert 2 * head_dim == LANES and head_dim == CHUNK and bp == 1
    d_attn = n_heads_a * 2 * head_dim
    n_heads_b, p_dim, n_state = state_ssm.shape[2:]
    d_ssd = n_heads_b * p_dim
    conv_dim_b = state_conv_ssd.shape[3]
    gn = (conv_dim_b - d_ssd) // 2
    n_groups = gn // n_state
    d_conv = state_conv_conf.shape[3]
    ssd_dims = (n_heads_b, p_dim, n_state, n_groups, ssd_conv_w.shape[1])
    src, width, order, cols, n_total, tn_proj = _layout(d_attn, d_ssd, gn, n_heads_b, d_conv)
    n_rope_blocks = 2 * d_attn // tn_proj

    tp, ts = bp * lp, bs * ls
    pos_p = jnp.arange(lp, dtype=I32)
    pos_s = past + jnp.arange(ls, dtype=I32)
    cos_p, sin_p = _rope_tables(pos_p)
    cos_s, sin_s = _rope_tables(jnp.tile(pos_s, bs))

    c_all = jnp.concatenate([c_prompt, c_sample], axis=0)
    c_rows = -(-c_all.shape[0] // (2 * SUBLANES)) * (2 * SUBLANES)
    mods = _adaln(jnp.pad(c_all, ((0, c_rows - c_all.shape[0]), (0, 0))), w_ada, b_ada)

    xp = x_prompt.reshape(tp, d)
    xs = x_sample.reshape(ts, d)
    zeros_ssm = jnp.zeros((bp, n_heads_b, p_dim, n_state), F32)
    zeros_cb = jnp.zeros((bp, ssd_conv_w.shape[1] - 1, conv_dim_b), F32)
    zeros_cc = jnp.zeros((bp, conf_dw_w.shape[1] - 1, d_conv), F32)
    caches = (jnp.transpose(cache_k, (0, 1, 3, 4, 5, 2)).reshape(depth, bs, d_attn, past),
              cache_v.reshape(depth, bs, past, d_attn))
    outs_p, outs_s = [], []

    for l in range(depth):
        lam0 = _lambda_init(l)
        w_in_f = _permute_w_in(w_in[l], src, width, order, n_total)
        weights = {BF16: (w_in_f.astype(BF16), w_out[l].astype(BF16)), F32: (w_in_f, w_out[l])}
        mod_p = [mods[l, 0:bp, j * d:(j + 1) * d] for j in range(6)]
        mod_s = [jnp.repeat(mods[l, bp:bp + bs, j * d:(j + 1) * d], ls, axis=0) for j in range(6)]
        if l % 2 == 0:
            ffn_f = (ffn_w1[l // 2], ffn_w3[l // 2], ffn_w2[l // 2])
            ffn_ws = {BF16: tuple(w.astype(BF16) for w in ffn_f), F32: ffn_f}

        def run(x, nb, ln, mod, cos, sin, caches, ssm0, conv_b0, conv_c0):
            sh1, sc1, g1, sh2, sc2, g2 = mod
            mm = BF16 if caches is None else F32
            w_in_l, w_out_l = weights[mm]
            h = _normmod(x, norm_mix[l], sc1, sh1, mm)
            if caches is None:
                proj, k_b, k_t, v_t = _proj(h, w_in_l, cos, sin, tn_proj, n_rope_blocks, (cols["k"], cols["v"]))
                proj3 = proj.reshape(nb, ln, n_total)
                ya = _flash_attention(proj, k_b, v_t, cols["q"], n_heads_a, attn_lambda[l], attn_subln[l], lam0)
                k_new = jnp.transpose(k_t.reshape(n_heads_a, 2, head_dim, nb, ln), (3, 4, 0, 1, 2))
            else:
                proj = _proj(h, w_in_l, cos, sin, tn_proj, n_rope_blocks)
                proj3 = proj.reshape(nb, ln, n_total)
                ya = _cached_attention(proj3, cols["q"], cols["k"], cols["v"], n_heads_a, caches[0], caches[1], l,
                                       attn_lambda[l], attn_subln[l], lam0).reshape(nb * ln, d_attn)
                k_new = proj3[:, :, cols["k"]:cols["k"] + d_attn].reshape(nb, ln, n_heads_a, 2, head_dim)
            yb, ssm_new, conv_b_new = _ssd(proj3, cols, ssd_dims, conv_b0, ssd_conv_w[l], ssd_conv_b[l],
                                           ssd_dt_bias[l], ssd_a_log[l], ssd_d[l], ssd_norm[l], ssm0, mm)
            yc, conv_c_new = _conf(proj3, cols["ga"], cols["gg"], d_conv, conv_c0, conf_dw_w[l], conf_dw_b[l],
                                   conf_ln_g[l], conf_ln_b[l], mm)
            x = _outproj(ya, yb.reshape(nb * ln, d_ssd), yc.reshape(nb * ln, d_conv), w_out_l, x, g1)
            if l % 2 == 0:
                h2 = _normmod(x, norm_ffn[l], sc2, sh2, mm)
                x = _ffn(h2, *ffn_ws[mm], x, g2)
            else:
                x = _moe(x, norm_ffn[l], sc2, sh2, g2, moe_router[l // 2], moe_w1, moe_w3, moe_w2, l // 2,
                         packed=caches is None)
            v_new = proj3[:, :, cols["v"]:cols["v"] + d_attn].reshape(nb, ln, n_heads_a, 2 * head_dim)
            return x, (k_new, v_new, ssm_new, conv_b_new, conv_c_new)

        xp, st_p = run(xp, bp, lp, mod_p, cos_p, sin_p, None, zeros_ssm, zeros_cb, zeros_cc)
        xs, st_s = run(xs, bs, ls, mod_s, cos_s, sin_s, caches, state_ssm[l], state_conv_ssd[l], state_conv_conf[l])
        outs_p.append(st_p)
        outs_s.append(st_s)

    y_prompt = _normmod(xp, norm_final, None, None, F32).reshape(bp, lp, d)
    y_sample = _normmod(xs, norm_final, None, None, F32).reshape(bs, ls, d)
    stack = lambda outs, k: jnp.stack([o[k] for o in outs])
    return (y_prompt, y_sample,
            stack(outs_p, 0), stack(outs_p, 1), stack(outs_p, 2), stack(outs_p, 3), stack(outs_p, 4),
            stack(outs_s, 0), stack(outs_s, 1), stack(outs_s, 2), stack(outs_s, 3), stack(outs_s, 4))
```

```python
import functools
import math

import numpy as np
import jax
import jax.numpy as jnp
from jax import lax
from jax.experimental import pallas as pl
from jax.experimental.pallas import tpu as pltpu

F32 = jnp.float32
BF16 = jnp.bfloat16
U32 = jnp.uint32
I32 = jnp.int32

CHUNK = 64
ROPE_THETA = 10000.0
EPS = 1e-6

LANES = 128
SUBLANES = 8
VMEM_LIMIT_BYTES = 48 * 1024 * 1024

ROW_TILE = 512
ATTN_TQ = 1024
ATTN_TK = 1024
ONES_ROWS = 16
LOG2_E = 1.4426950408889634
SSD_CHUNK = 128
CONF_TILE = 256
EXPERT_TILE = 512
GATHER_TILE = 256


def _cparams(*sem):
    return pltpu.CompilerParams(dimension_semantics=sem, vmem_limit_bytes=VMEM_LIMIT_BYTES)


def _tile(n, pref):
    t = min(n, pref)
    assert n % t == 0, (n, pref)
    return t


def _rows_spec(arr, tm, width, by_col=False):
    if arr.shape[0] == 1:
        return pl.BlockSpec((1, width), lambda j, i: (0, j if by_col else 0))
    return pl.BlockSpec((tm, width), lambda j, i: (i, j if by_col else 0))


def _silu(x):
    return x * jax.nn.sigmoid(x)


def _precision(a, b):
    assert a.dtype == b.dtype, (a.dtype, b.dtype)
    return lax.Precision.HIGHEST if a.dtype == F32 else None


def _dot(a, b):
    return jnp.dot(a, b, preferred_element_type=F32, precision=_precision(a, b))


def _dot_nt(a, b):
    return lax.dot_general(a, b, (((1,), (1,)), ((), ())), preferred_element_type=F32, precision=_precision(a, b))


def _adaln_kernel(c_ref, w_ref, b_ref, o_ref):
    o_ref[0] = _dot(_silu(c_ref[...]), w_ref[0]) + b_ref[0]


def _adaln(c_all, w_ada, b_ada):
    depth, d, mc = w_ada.shape
    rows = c_all.shape[0]
    tn = _tile(mc, 1024)
    return pl.pallas_call(
        _adaln_kernel,
        out_shape=jax.ShapeDtypeStruct((depth, rows, mc), F32),
        grid=(depth, mc // tn),
        in_specs=[pl.BlockSpec((rows, d), lambda l, j: (0, 0)),
                  pl.BlockSpec((1, d, tn), lambda l, j: (l, 0, j)),
                  pl.BlockSpec((1, 1, tn), lambda l, j: (l, 0, j))],
        out_specs=pl.BlockSpec((1, rows, tn), lambda l, j: (l, 0, j)),
        compiler_params=_cparams("parallel", "parallel"),
        name="adaln",
    )(c_all, w_ada, b_ada.reshape(depth, 1, mc))


def _modulated(x, g, sc, sh):
    ms = jnp.mean(x * x, axis=-1, keepdims=True)
    y = x * lax.rsqrt(ms + EPS) * g
    if sc is not None:
        y = y * (1.0 + sc) + sh
    return y


def _normmod_kernel(x_ref, g_ref, sc_ref, sh_ref, o_ref):
    o_ref[...] = _modulated(x_ref[...], g_ref[...], sc_ref[...], sh_ref[...]).astype(o_ref.dtype)


def _norm_kernel(x_ref, g_ref, o_ref):
    o_ref[...] = _modulated(x_ref[...], g_ref[...], None, None).astype(o_ref.dtype)


def _normmod(x, g, sc, sh, out_dtype):
    t, d = x.shape
    tm = _tile(t, ROW_TILE)
    g = g.reshape(1, d)
    row = pl.BlockSpec((tm, d), lambda j, i: (i, 0))
    vec = pl.BlockSpec((1, d), lambda j, i: (0, 0))
    if sc is None:
        body, ops, specs = _norm_kernel, (x, g), [row, vec]
    else:
        body, ops = _normmod_kernel, (x, g, sc, sh)
        specs = [row, vec, _rows_spec(sc, tm, d), _rows_spec(sh, tm, d)]
    return pl.pallas_call(
        body, out_shape=jax.ShapeDtypeStruct((t, d), out_dtype),
        grid=(1, t // tm), in_specs=specs, out_specs=row,
        compiler_params=_cparams("parallel", "parallel"), name="normmod",
    )(*ops)


def _proj_kernel(a_ref, w_ref, cos_ref, sin_ref, o_ref, *extra_refs, n_rope_blocks, jk, jv):
    acc = _dot(a_ref[...], w_ref[...])
    j = pl.program_id(1)

    @pl.when(j >= n_rope_blocks)
    def _():
        o_ref[...] = acc

    @pl.when(j < n_rope_blocks)
    def _():
        cos = cos_ref[...]
        sin = sin_ref[...]
        lane = lax.broadcasted_iota(I32, cos.shape, 1)
        first_half = (lane % CHUNK) < (CHUNK // 2)
        for c in range(acc.shape[1] // LANES):
            blk = acc[:, c * LANES:(c + 1) * LANES]
            partner = jnp.where(first_half, pltpu.roll(blk, LANES - 32, 1), pltpu.roll(blk, 32, 1))
            o_ref[:, c * LANES:(c + 1) * LANES] = blk * cos + partner * sin

    if extra_refs:
        kb_ref, kt_ref, vt_ref = extra_refs

        @pl.when(j == jk)
        def _():
            k = o_ref[...]
            kb_ref[...] = k.astype(BF16)
            kt_ref[...] = k.T

        @pl.when(j == jv)
        def _():
            vt_ref[...] = acc.T.astype(BF16)


def _proj(h, w, cos, sin, tn, n_rope_blocks, extras=None):
    t, d = h.shape
    n = w.shape[1]
    tm = _tile(t, ROW_TILE)
    out_shape = [jax.ShapeDtypeStruct((t, n), F32)]
    out_specs = [pl.BlockSpec((tm, tn), lambda i, j: (i, j))]
    jk = jv = -1
    if extras is not None:
        assert extras[0] % tn == 0 and extras[1] % tn == 0
        jk, jv = extras[0] // tn, extras[1] // tn
        out_shape += [jax.ShapeDtypeStruct((t, tn), BF16), jax.ShapeDtypeStruct((tn, t), F32),
                      jax.ShapeDtypeStruct((tn, t), BF16)]
        out_specs += [pl.BlockSpec((tm, tn), lambda i, j: (i, 0)), pl.BlockSpec((tn, tm), lambda i, j: (0, i)),
                      pl.BlockSpec((tn, tm), lambda i, j: (0, i))]
    outs = pl.pallas_call(
        functools.partial(_proj_kernel, n_rope_blocks=n_rope_blocks, jk=jk, jv=jv),
        out_shape=out_shape,
        grid=(t // tm, n // tn),
        in_specs=[pl.BlockSpec((tm, d), lambda i, j: (i, 0)),
                  pl.BlockSpec((d, tn), lambda i, j: (0, j)),
                  pl.BlockSpec((tm, LANES), lambda i, j: (i, 0)),
                  pl.BlockSpec((tm, LANES), lambda i, j: (i, 0))],
        out_specs=out_specs,
        compiler_params=_cparams("parallel", "arbitrary"), name="proj",
    )(h, w, cos, sin)
    return outs if extras is not None else outs[0]


def _lambda_value(lam_ref, lambda_init):
    lp = lam_ref[...]
    a = jnp.sum(lp[0:1] * lp[1:2], axis=-1, keepdims=True)
    b = jnp.sum(lp[2:3] * lp[3:4], axis=-1, keepdims=True)
    return jnp.exp(a) - jnp.exp(b) + lambda_init


def _stack_maps(q):
    lane = lax.broadcasted_iota(I32, q.shape, 1)
    lo = lane < (LANES // 2)
    return jnp.concatenate([jnp.where(lo, q, 0.0), jnp.where(lo, 0.0, q)], axis=0)


def _diff_finish(o1, o2, lam, g, lambda_init):
    o = o1 - lam * o2
    ms = jnp.mean(o * o, axis=-1, keepdims=True)
    return o * lax.rsqrt(ms + EPS) * g * (1.0 - lambda_init)


def _flash_kernel(qi_ref, ki_ref, q_ref, k_ref, vtp_ref, vt_ref, lam_ref, g_ref, o_ref,
                  qst_ref, m_ref, acc_ref, pp_ref, ap_ref, *, tq, tk, lambda_init):
    p = pl.program_id(1)
    qi = qi_ref[p]
    ki = ki_ref[p]
    ones = jnp.ones((ONES_ROWS, tk), BF16)

    @pl.when(ki == 0)
    def _():
        qt = (q_ref[...] * (float(CHUNK) ** -0.5 * LOG2_E)).T
        lo = lax.broadcasted_iota(I32, qt.shape, 0) < (LANES // 2)
        qst_ref[:, 0:tq] = jnp.where(lo, qt, 0.0).astype(BF16)
        qst_ref[:, tq:2 * tq] = jnp.where(lo, 0.0, qt).astype(BF16)
        m_ref[...] = jnp.full(m_ref.shape, -jnp.inf, F32)
        acc_ref[...] = jnp.zeros(acc_ref.shape, F32)
        pp_ref[...] = jnp.zeros(pp_ref.shape, BF16)
        ap_ref[...] = jnp.zeros(ap_ref.shape, F32)

    def step(masked):
        st = _dot(k_ref[...], qst_ref[...])
        if masked:
            k_chunk = (ki * tk + lax.broadcasted_iota(I32, st.shape, 0)) // CHUNK
            q_chunk = (qi * tq + lax.broadcasted_iota(I32, st.shape, 1) % tq) // CHUNK
            st = jnp.where(k_chunk <= q_chunk, st, -jnp.inf)
        prev = _dot(jnp.concatenate([vtp_ref[...], ones], axis=0), pp_ref[...])
        acc_ref[...] = ap_ref[...] * acc_ref[...] + prev
        m_prev = m_ref[...]
        m_new = jnp.maximum(m_prev, jnp.max(st, axis=0, keepdims=True))
        ap_ref[...] = jnp.exp2(m_prev - m_new)
        pp_ref[...] = jnp.exp2((st - m_new).astype(BF16))
        m_ref[...] = m_new

    needs_mask = (ki + 1) * tk > qi * tq

    @pl.when(jnp.logical_not(needs_mask))
    def _():
        step(False)

    @pl.when(needs_mask)
    def _():
        step(True)

    @pl.when((ki + 1) * tk == (qi + 1) * tq)
    def _():
        last = _dot(jnp.concatenate([vt_ref[...], ones], axis=0), pp_ref[...])
        acc = ap_ref[...] * acc_ref[...] + last
        o = acc[0:LANES, :] * (1.0 / acc[LANES:LANES + 1, :])
        lam = _lambda_value(lam_ref, lambda_init)
        o = (o[:, 0:tq] - lam * o[:, tq:2 * tq]).T
        ms = jnp.mean(o * o, axis=-1, keepdims=True)
        o_ref[...] = (o * lax.rsqrt(ms + EPS) * g_ref[...] * (1.0 - lambda_init)).astype(o_ref.dtype)


def _flash_attention(proj, k_b, v_t, col_q, n_heads, lam_p, subln, lambda_init):
    t = proj.shape[0]
    tq = _tile(t, ATTN_TQ)
    tk = _tile(tq, ATTN_TK)
    assert tk % CHUNK == 0
    per_q = tq // tk
    pairs = [(a, b) for a in range(t // tq) for b in range((a + 1) * per_q)]
    qi = jnp.asarray(np.array([a for a, _ in pairs], np.int32))
    ki = jnp.asarray(np.array([b for _, b in pairs], np.int32))
    cq = col_q // LANES
    grid_spec = pltpu.PrefetchScalarGridSpec(
        num_scalar_prefetch=2,
        grid=(n_heads, len(pairs)),
        in_specs=[pl.BlockSpec((tq, LANES), lambda h, p, qi, ki: (qi[p], cq + h)),
                  pl.BlockSpec((tk, LANES), lambda h, p, qi, ki: (ki[p], h)),
                  pl.BlockSpec((LANES, tk), lambda h, p, qi, ki: (h, jnp.maximum(ki[p] - 1, 0))),
                  pl.BlockSpec((LANES, tk), lambda h, p, qi, ki: (h, ki[p])),
                  pl.BlockSpec(lam_p.shape, lambda h, p, qi, ki: (0, 0)),
                  pl.BlockSpec((1, LANES), lambda h, p, qi, ki: (0, 0))],
        out_specs=pl.BlockSpec((tq, LANES), lambda h, p, qi, ki: (qi[p], h)),
        scratch_shapes=[pltpu.VMEM((LANES, 2 * tq), BF16),
                        pltpu.VMEM((1, 2 * tq), F32),
                        pltpu.VMEM((LANES + ONES_ROWS, 2 * tq), F32),
                        pltpu.VMEM((tk, 2 * tq), BF16),
                        pltpu.VMEM((1, 2 * tq), F32)])
    return pl.pallas_call(
        functools.partial(_flash_kernel, tq=tq, tk=tk, lambda_init=lambda_init),
        out_shape=jax.ShapeDtypeStruct((t, n_heads * LANES), BF16),
        grid_spec=grid_spec,
        compiler_params=_cparams("parallel", "arbitrary"), name="flash_attn",
    )(qi, ki, proj, k_b, v_t, v_t, lam_p, subln.reshape(1, LANES))


def _cached_attn_kernel(q_ref, kn_ref, vn_ref, kct_ref, vc_ref, lam_ref, g_ref, o_ref, *, lambda_init):
    ln = q_ref.shape[0]
    past = kct_ref.shape[1]
    qs = _stack_maps(q_ref[...] * (float(CHUNK) ** -0.5))
    sc = _dot(qs, kct_ref[...])
    sn = _dot_nt(qs, kn_ref[...])
    q_chunk_c = (past + lax.broadcasted_iota(I32, sc.shape, 0) % ln) // CHUNK
    q_chunk_n = (past + lax.broadcasted_iota(I32, sn.shape, 0) % ln) // CHUNK
    sc = jnp.where(lax.broadcasted_iota(I32, sc.shape, 1) // CHUNK <= q_chunk_c, sc, -jnp.inf)
    sn = jnp.where((past + lax.broadcasted_iota(I32, sn.shape, 1)) // CHUNK <= q_chunk_n, sn, -jnp.inf)
    m = jnp.maximum(jnp.max(sc, axis=1, keepdims=True), jnp.max(sn, axis=1, keepdims=True))
    pc = jnp.exp(sc - m)
    pn = jnp.exp(sn - m)
    denom = jnp.sum(pc, axis=1, keepdims=True) + jnp.sum(pn, axis=1, keepdims=True)
    o = (_dot(pc, vc_ref[...]) + _dot(pn, vn_ref[...])) / denom
    lam = _lambda_value(lam_ref, lambda_init)
    o_ref[...] = _diff_finish(o[0:ln], o[ln:2 * ln], lam, g_ref[...], lambda_init).astype(o_ref.dtype)


def _cached_attention(proj3, col_q, col_k, col_v, n_heads, k_cache_t, v_cache, li, lam_p, subln, lambda_init):
    b, ln, _ = proj3.shape
    past = v_cache.shape[2]
    cq, ck, cv = col_q // LANES, col_k // LANES, col_v // LANES
    new = lambda c: pl.BlockSpec((None, ln, LANES), lambda bi, h: (bi, 0, c + h))
    return pl.pallas_call(
        functools.partial(_cached_attn_kernel, lambda_init=lambda_init),
        out_shape=jax.ShapeDtypeStruct((b, ln, n_heads * LANES), F32),
        grid=(b, n_heads),
        in_specs=[new(cq), new(ck), new(cv),
                  pl.BlockSpec((None, None, LANES, past), lambda bi, h: (li, bi, h, 0)),
                  pl.BlockSpec((None, None, past, LANES), lambda bi, h: (li, bi, 0, h)),
                  pl.BlockSpec(lam_p.shape, lambda bi, h: (0, 0)),
                  pl.BlockSpec((1, LANES), lambda bi, h: (0, 0))],
        out_specs=pl.BlockSpec((None, ln, LANES), lambda bi, h: (bi, 0, h)),
        compiler_params=_cparams("parallel", "parallel"), name="cached_attn",
    )(proj3, proj3, proj3, k_cache_t, v_cache, lam_p, subln.reshape(1, LANES))


def _split3(x):
    x1 = x.astype(BF16)
    r1 = x - x1.astype(F32)
    x2 = r1.astype(BF16)
    x3 = (r1 - x2.astype(F32)).astype(BF16)
    return x1, x2, x3


def _ssd_kernel(xs_ref, bm_ref, cm_ref, hx_ref, hb_ref, hc_ref, sx_ref, sb_ref, sc_ref,
                wx_ref, wb_ref, wc_ref, bx_ref, bb_ref, bc_ref,
                dt_ref, dtb_ref, alog_ref, z_ref, dexp_ref, gn_ref, h0_ref,
                y_ref, hout_ref, nsx_ref, nsb_ref, nsc_ref,
                scr_x, scr_b, scr_c, h_scr, y_scr, xte_scr,
                *, q, qp, n_heads, p_dim, n_state, n_groups, width):
    i = pl.program_id(1)
    first = i == 0
    halo = SUBLANES

    @pl.when(first)
    def _():
        h_scr[...] = h0_ref[...]

    def conv(t_ref, halo_ref, st_ref, w_ref, b_ref, scr, ns_ref):
        scr[0:halo] = jnp.where(first, st_ref[...], halo_ref[...])
        scr[halo:halo + q] = t_ref[...]
        acc = b_ref[...] + w_ref[0:1, :] * scr[halo - width + 1:halo - width + 1 + q]
        for k in range(1, width):
            acc = acc + w_ref[k:k + 1, :] * scr[halo - width + 1 + k:halo - width + 1 + k + q]
        ns_ref[...] = scr[q:q + halo]
        return _silu(acc)

    xs = conv(xs_ref, hx_ref, sx_ref, wx_ref, bx_ref, scr_x, nsx_ref)
    bm = conv(bm_ref, hb_ref, sb_ref, wb_ref, bb_ref, scr_b, nsb_ref)
    cm = conv(cm_ref, hc_ref, sc_ref, wc_ref, bc_ref, scr_c, nsc_ref)
    dt = jax.nn.softplus(dt_ref[...] + dtb_ref[...])
    a = -jnp.exp(alog_ref[...])

    def pad(v):
        if qp == q:
            return v
        return jnp.concatenate([v, jnp.zeros((qp - q, v.shape[1]), v.dtype)], axis=0)

    xs_p, bm_p, cm_p, dt_p = pad(xs), pad(bm), pad(cm), pad(dt)
    d_a = dt_p * a
    row = lax.broadcasted_iota(I32, (qp, qp), 0)
    col = lax.broadcasted_iota(I32, (qp, qp), 1)
    tril = row >= col
    ones_tril = jnp.where(tril, 1.0, 0.0).astype(BF16)
    d1, d2, d3 = _split3(d_a)
    acum = _dot(ones_tril, d1) + _dot(ones_tril, d2) + _dot(ones_tril, d3)
    acum_t = acum.T
    total = acum[qp - 1:qp, :]
    mm = y_ref.dtype
    bm_b = bm_p.astype(mm)
    cm_b = cm_p.astype(mm)
    scores = [_dot_nt(cm_b[:, g * n_state:(g + 1) * n_state], bm_b[:, g * n_state:(g + 1) * n_state])
              for g in range(n_groups)]
    rep = n_heads // n_groups
    for h in range(n_heads):
        g = h // rep
        colv = acum[:, h:h + 1]
        rowv = acum_t[h:h + 1, :]
        tot = total[:, h:h + 1]
        decay = jnp.where(tril, jnp.exp(colv - rowv), 0.0)
        xdt = xs_p[:, h * p_dim:(h + 1) * p_dim] * dt_p[:, h:h + 1]
        y_diag = _dot((scores[g] * decay).astype(mm), xdt.astype(mm))
        h_prev = h_scr[h]
        y_off = jnp.exp(colv) * _dot_nt(cm_b[:, g * n_state:(g + 1) * n_state], h_prev.astype(mm))
        y_scr[:, h * p_dim:(h + 1) * p_dim] = y_diag + y_off
        xte_scr[:, h * p_dim:(h + 1) * p_dim] = xdt * jnp.exp(tot - colv)
    xte_t = xte_scr[...].T.astype(mm)
    for h in range(n_heads):
        g = h // rep
        st = _dot(xte_t[h * p_dim:(h + 1) * p_dim, :], bm_b[:, g * n_state:(g + 1) * n_state])
        h_scr[h] = jnp.exp(total[:, h:h + 1]) * h_scr[h] + st
    y = y_scr[0:q] + dexp_ref[...] * xs
    yg = y * _silu(z_ref[...])
    ms = jnp.mean(yg * yg, axis=-1, keepdims=True)
    y_ref[...] = (yg * lax.rsqrt(ms + EPS) * gn_ref[...]).astype(y_ref.dtype)
    hout_ref[...] = h_scr[...]


def _ssd(proj3, cols, dims, conv_state, conv_w, conv_b, dt_bias, a_log, d_skip, norm_g, h0, out_dtype):
    b, ln, _ = proj3.shape
    n_heads, p_dim, n_state, n_groups, width = dims
    d_ssd = n_heads * p_dim
    gn = n_groups * n_state
    q = _tile(ln, SSD_CHUNK)
    qp = max(q, LANES)
    assert q % SUBLANES == 0 and ln >= width - 1 and width - 1 <= SUBLANES and n_heads <= LANES
    nsteps = ln // q
    halo = SUBLANES
    widths = (d_ssd, gn, gn)
    offs = (cols["xs"], cols["bm"], cols["cm"])
    for o, w in zip(offs, widths):
        assert o % w == 0
    assert cols["z"] % d_ssd == 0 and cols["dt"] % LANES == 0

    def tile_spec(o, w):
        return pl.BlockSpec((None, q, w), lambda bi, i: (bi, i, o // w))

    def halo_spec(o, w):
        return pl.BlockSpec((None, halo, w), lambda bi, i: (bi, jnp.maximum(i * (q // halo) - 1, 0), o // w))

    def state_spec(w):
        return pl.BlockSpec((None, halo, w), lambda bi, i: (bi, 0, 0))

    def const_spec(shape):
        return pl.BlockSpec(shape, lambda bi, i: (0,) * len(shape))

    st = jnp.pad(conv_state, ((0, 0), (halo - (width - 1), 0), (0, 0)))
    bounds = (0, d_ssd, d_ssd + gn, d_ssd + 2 * gn)
    st_segs = [st[..., bounds[k]:bounds[k + 1]] for k in range(3)]
    w_segs = [conv_w[:, bounds[k]:bounds[k + 1]] for k in range(3)]
    b_segs = [conv_b[bounds[k]:bounds[k + 1]].reshape(1, -1) for k in range(3)]
    pad_lane = lambda v: jnp.pad(v.reshape(1, -1), ((0, 0), (0, LANES - v.shape[0])))
    in_specs = ([tile_spec(o, w) for o, w in zip(offs, widths)]
                + [halo_spec(o, w) for o, w in zip(offs, widths)]
                + [state_spec(w) for w in widths]
                + [const_spec((width, w)) for w in widths]
                + [const_spec((1, w)) for w in widths]
                + [tile_spec(cols["dt"], LANES), const_spec((1, LANES)), const_spec((1, LANES)),
                   tile_spec(cols["z"], d_ssd), const_spec((1, d_ssd)), const_spec((1, d_ssd)),
                   pl.BlockSpec((None, n_heads, p_dim, n_state), lambda bi, i: (bi, 0, 0, 0))])
    out_shape = (jax.ShapeDtypeStruct((b, ln, d_ssd), out_dtype),
                 jax.ShapeDtypeStruct((b, n_heads, p_dim, n_state), F32),
                 jax.ShapeDtypeStruct((b, halo, d_ssd), F32),
                 jax.ShapeDtypeStruct((b, halo, gn), F32),
                 jax.ShapeDtypeStruct((b, halo, gn), F32))
    out_specs = (pl.BlockSpec((None, q, d_ssd), lambda bi, i: (bi, i, 0)),
                 pl.BlockSpec((None, n_heads, p_dim, n_state), lambda bi, i: (bi, 0, 0, 0)),
                 state_spec(d_ssd), state_spec(gn), state_spec(gn))
    scratch = [pltpu.VMEM((q + halo, d_ssd), F32), pltpu.VMEM((q + halo, gn), F32), pltpu.VMEM((q + halo, gn), F32),
               pltpu.VMEM((n_heads, p_dim, n_state), F32),
               pltpu.VMEM((qp, d_ssd), F32), pltpu.VMEM((qp, d_ssd), F32)]
    y, h_new, nsx, nsb, nsc = pl.pallas_call(
        functools.partial(_ssd_kernel, q=q, qp=qp, n_heads=n_heads, p_dim=p_dim, n_state=n_state,
                          n_groups=n_groups, width=width),
        out_shape=out_shape, grid=(b, nsteps), in_specs=in_specs, out_specs=out_specs,
        scratch_shapes=scratch,
        compiler_params=_cparams("parallel", "arbitrary"), name="ssd",
    )(proj3, proj3, proj3, proj3, proj3, proj3, *st_segs, *w_segs, *b_segs,
      proj3, pad_lane(dt_bias), pad_lane(a_log), proj3,
      jnp.repeat(d_skip, p_dim).reshape(1, d_ssd), norm_g.reshape(1, d_ssd), h0)
    new_state = jnp.concatenate([nsx, nsb, nsc], axis=-1)[:, halo - (width - 1):, :]
    return y, h_new, new_state


def _conf_kernel(ga_ref, gg_ref, hga_ref, hgg_ref, st_ref, w_ref, b_ref, lg_ref, lb_ref,
                 y_ref, ns_ref, scr, *, tb, width, halo):
    i = pl.program_id(1)
    u = ga_ref[...] * jax.nn.sigmoid(gg_ref[...])
    uh = hga_ref[...] * jax.nn.sigmoid(hgg_ref[...])
    scr[0:halo] = jnp.where(i == 0, st_ref[...], uh)
    scr[halo:halo + tb] = u
    base = halo - (width - 1)
    acc = b_ref[...] + w_ref[0:1, :] * scr[base:base + tb]
    for k in range(1, width):
        acc = acc + w_ref[k:k + 1, :] * scr[base + k:base + k + tb]
    mu = jnp.mean(acc, axis=-1, keepdims=True)
    cen = acc - mu
    var = jnp.mean(cen * cen, axis=-1, keepdims=True)
    y = cen * lax.rsqrt(var + EPS) * lg_ref[...] + lb_ref[...]
    y_ref[...] = _silu(y).astype(y_ref.dtype)
    ns_ref[...] = scr[tb:tb + halo]


def _conf(proj3, col_ga, col_gg, d_conv, conv_state, w, bias, ln_g, ln_b, out_dtype):
    b, ln, _ = proj3.shape
    width = w.shape[0]
    halo = -(-(width - 1) // SUBLANES) * SUBLANES
    tb = _tile(ln, CONF_TILE)
    assert tb % halo == 0 and ln >= width - 1 and col_ga % d_conv == 0 and col_gg % d_conv == 0
    st = jnp.pad(conv_state, ((0, 0), (halo - (width - 1), 0), (0, 0)))
    tile_spec = lambda o: pl.BlockSpec((None, tb, d_conv), lambda bi, i: (bi, i, o // d_conv))
    halo_spec = lambda o: pl.BlockSpec((None, halo, d_conv),
                                       lambda bi, i: (bi, jnp.maximum(i * (tb // halo) - 1, 0), o // d_conv))
    st_spec = pl.BlockSpec((None, halo, d_conv), lambda bi, i: (bi, 0, 0))
    vec = pl.BlockSpec((1, d_conv), lambda bi, i: (0, 0))
    y, ns = pl.pallas_call(
        functools.partial(_conf_kernel, tb=tb, width=width, halo=halo),
        out_shape=(jax.ShapeDtypeStruct((b, ln, d_conv), out_dtype), jax.ShapeDtypeStruct((b, halo, d_conv), F32)),
        grid=(b, ln // tb),
        in_specs=[tile_spec(col_ga), tile_spec(col_gg), halo_spec(col_ga), halo_spec(col_gg), st_spec,
                  pl.BlockSpec((width, d_conv), lambda bi, i: (0, 0)), vec, vec, vec],
        out_specs=(pl.BlockSpec((None, tb, d_conv), lambda bi, i: (bi, i, 0)), st_spec),
        scratch_shapes=[pltpu.VMEM((tb + halo, d_conv), F32)],
        compiler_params=_cparams("parallel", "arbitrary"), name="conf_conv",
    )(proj3, proj3, proj3, proj3, st, w, bias.reshape(1, -1), ln_g.reshape(1, -1), ln_b.reshape(1, -1))
    return y, ns[:, halo - (width - 1):, :]


def _outproj_kernel(ya_ref, yb_ref, yc_ref, wa_ref, wb_ref, wc_ref, res_ref, gate_ref, o_ref):
    acc = _dot(ya_ref[...], wa_ref[...]) + _dot(yb_ref[...], wb_ref[...]) + _dot(yc_ref[...], wc_ref[...])
    o_ref[...] = res_ref[...] + gate_ref[...] * acc


def _outproj(ya, yb, yc, w_out, res, gate):
    t, d = res.shape
    da, db = ya.shape[1], yb.shape[1]
    tm = _tile(t, ROW_TILE)
    tn = _tile(d, 1024)
    a_spec = lambda k: pl.BlockSpec((tm, k), lambda j, i: (i, 0))
    w_spec = lambda k: pl.BlockSpec((k, tn), lambda j, i: (0, j))
    return pl.pallas_call(
        _outproj_kernel, out_shape=jax.ShapeDtypeStruct((t, d), F32),
        grid=(d // tn, t // tm),
        in_specs=[a_spec(da), a_spec(db), a_spec(yc.shape[1]), w_spec(da), w_spec(db), w_spec(yc.shape[1]),
                  pl.BlockSpec((tm, tn), lambda j, i: (i, j)), _rows_spec(gate, tm, tn, by_col=True)],
        out_specs=pl.BlockSpec((tm, tn), lambda j, i: (i, j)),
        compiler_params=_cparams("parallel", "parallel"), name="outproj",
    )(ya, yb, yc, w_out[:da], w_out[da:da + db], w_out[da + db:], res, gate)


def _ffn_up_kernel(a_ref, w1_ref, w3_ref, o_ref):
    a = a_ref[...]
    o_ref[...] = (_silu(_dot(a, w1_ref[...])) * _dot(a, w3_ref[...])).astype(o_ref.dtype)


def _ffn_down_kernel(a_ref, w_ref, res_ref, gate_ref, o_ref):
    o_ref[...] = res_ref[...] + gate_ref[...] * _dot(a_ref[...], w_ref[...])


def _ffn(h, w1, w3, w2, res, gate):
    t, d = h.shape
    f = w1.shape[1]
    tm = _tile(t, ROW_TILE)
    tn = _tile(f, 512)
    up = pl.pallas_call(
        _ffn_up_kernel, out_shape=jax.ShapeDtypeStruct((t, f), h.dtype),
        grid=(f // tn, t // tm),
        in_specs=[pl.BlockSpec((tm, d), lambda j, i: (i, 0)),
                  pl.BlockSpec((d, tn), lambda j, i: (0, j)),
                  pl.BlockSpec((d, tn), lambda j, i: (0, j))],
        out_specs=pl.BlockSpec((tm, tn), lambda j, i: (i, j)),
        compiler_params=_cparams("parallel", "parallel"), name="ffn_up",
    )(h, w1, w3)
    tn2 = _tile(d, 512)
    return pl.pallas_call(
        _ffn_down_kernel, out_shape=jax.ShapeDtypeStruct((t, d), F32),
        grid=(d // tn2, t // tm),
        in_specs=[pl.BlockSpec((tm, f), lambda j, i: (i, 0)),
                  pl.BlockSpec((f, tn2), lambda j, i: (0, j)),
                  pl.BlockSpec((tm, tn2), lambda j, i: (i, j)),
                  _rows_spec(gate, tm, tn2, by_col=True)],
        out_specs=pl.BlockSpec((tm, tn2), lambda j, i: (i, j)),
        compiler_params=_cparams("parallel", "parallel"), name="ffn_down",
    )(up, w2, res, gate)


def _router_kernel(x_ref, g_ref, sc_ref, sh_ref, r_ref, hp_ref, sel_ref, gate_ref, tot_ref, *, n_experts):
    h = _modulated(x_ref[...], g_ref[...], sc_ref[...], sh_ref[...])
    if hp_ref.dtype == U32:
        half = h.shape[1] // 2
        lo = lax.bitcast_convert_type(h[:, :half].astype(BF16).astype(F32), U32)
        hi = lax.bitcast_convert_type(h[:, half:].astype(BF16).astype(F32), U32)
        hp_ref[...] = (lo >> 16) | hi
    else:
        hp_ref[...] = h
    r = r_ref[...]
    h1 = h.astype(BF16)
    h2 = (h - h1.astype(F32)).astype(BF16)
    r1 = r.astype(BF16)
    r2 = (r - r1.astype(F32)).astype(BF16)
    logits = _dot(h1, r1) + _dot(h1, r2) + _dot(h2, r1)
    lane = lax.broadcasted_iota(I32, logits.shape, 1)
    lane_f = lane.astype(F32)
    lg = jnp.where(lane < n_experts, logits, -jnp.inf)
    v1 = jnp.max(lg, axis=1, keepdims=True)
    i1 = jnp.min(jnp.where(lg == v1, lane_f, float(LANES)), axis=1, keepdims=True)
    lg2 = jnp.where(lane_f == i1, -jnp.inf, lg)
    v2 = jnp.max(lg2, axis=1, keepdims=True)
    i2 = jnp.min(jnp.where(lg2 == v2, lane_f, float(LANES)), axis=1, keepdims=True)
    e = jnp.exp(v2 - v1)
    g1 = 1.0 / (1.0 + e)
    g2 = e * g1
    first = lane_f == i1
    second = lane_f == i2
    sel_ref[...] = jnp.where(first, 1, jnp.where(second, 2, 0)).astype(I32)
    gate_ref[...] = jnp.where(lane == 0, g1, jnp.where(lane == 1, g2, 0.0))
    cnt = jnp.sum(jnp.where(first | second, 1.0, 0.0), axis=0, keepdims=True)

    @pl.when(pl.program_id(1) == 0)
    def _():
        tot_ref[...] = jnp.zeros(tot_ref.shape, F32)

    tot_ref[...] += cnt


def _router(x, g, sc, sh, router_w, packed):
    t, d = x.shape
    n_experts = router_w.shape[1]
    tm = _tile(t, ROW_TILE)
    r = jnp.pad(router_w, ((0, 0), (0, LANES - n_experts)))
    row = lambda w: pl.BlockSpec((tm, w), lambda j, i: (i, 0))
    hw, hdt = (d // 2, U32) if packed else (d, F32)
    return pl.pallas_call(
        functools.partial(_router_kernel, n_experts=n_experts),
        out_shape=(jax.ShapeDtypeStruct((t, hw), hdt), jax.ShapeDtypeStruct((t, LANES), I32),
                   jax.ShapeDtypeStruct((t, LANES), F32), jax.ShapeDtypeStruct((1, LANES), F32)),
        grid=(1, t // tm),
        in_specs=[row(d), pl.BlockSpec((1, d), lambda j, i: (0, 0)), _rows_spec(sc, tm, d), _rows_spec(sh, tm, d),
                  pl.BlockSpec((d, LANES), lambda j, i: (0, 0))],
        out_specs=(row(hw), row(LANES), row(LANES), pl.BlockSpec((1, LANES), lambda j, i: (0, 0))),
        compiler_params=_cparams("arbitrary", "arbitrary"), name="router",
    )(x, g.reshape(1, d), sc, sh, r)


def _route_kernel(sel_ref, off_ref, pos_ref, carry_ref):
    @pl.when(pl.program_id(0) == 0)
    def _():
        carry_ref[...] = jnp.zeros(carry_ref.shape, F32)

    s = sel_ref[...]
    tm = s.shape[0]
    cnt = jnp.where(s > 0, 1.0, 0.0)
    row = lax.broadcasted_iota(I32, (tm, tm), 0)
    col = lax.broadcasted_iota(I32, (tm, tm), 1)
    before = jnp.where(row > col, 1.0, 0.0).astype(BF16)
    dest = _dot(before, cnt.astype(BF16)) + carry_ref[...] + off_ref[...]
    p1 = jnp.sum(jnp.where(s == 1, dest, 0.0), axis=1, keepdims=True)
    p2 = jnp.sum(jnp.where(s == 2, dest, 0.0), axis=1, keepdims=True)
    lane = lax.broadcasted_iota(I32, s.shape, 1)
    pos_ref[...] = jnp.where(lane == 0, p1, jnp.where(lane == 1, p2, 0.0)).astype(I32)
    carry_ref[...] += jnp.sum(cnt, axis=0, keepdims=True)


def _route(sel, offsets):
    t = sel.shape[0]
    tm = _tile(t, ROW_TILE)
    return pl.pallas_call(
        _route_kernel, out_shape=jax.ShapeDtypeStruct((t, LANES), I32),
        grid=(t // tm,),
        in_specs=[pl.BlockSpec((tm, LANES), lambda i: (i, 0)), pl.BlockSpec((1, LANES), lambda i: (0, 0))],
        out_specs=pl.BlockSpec((tm, LANES), lambda i: (i, 0)),
        scratch_shapes=[pltpu.VMEM((1, LANES), F32)],
        compiler_params=_cparams("arbitrary"), name="route",
    )(sel, offsets)


def _dispatch_kernel(p1_ref, p2_ref, hp_ref, init_ref, xs_ref, sem, *, tr):
    del init_ref
    base = pl.program_id(0) * tr

    def copy(r, dst):
        return pltpu.make_async_copy(hp_ref.at[pl.ds(r, 1), :], xs_ref.at[pl.ds(dst, 1), :], sem)

    def issue(r, c):
        copy(r, p1_ref[base + r]).start()
        copy(r, p2_ref[base + r]).start()
        return c

    def drain(r, c):
        copy(r, p1_ref[base + r]).wait()
        copy(r, p2_ref[base + r]).wait()
        return c

    lax.fori_loop(0, tr, issue, 0)
    lax.fori_loop(0, tr, drain, 0)


def _dispatch(hp, pos1, pos2, n_rows):
    t, w = hp.shape
    tr = _tile(t, GATHER_TILE)
    grid_spec = pltpu.PrefetchScalarGridSpec(
        num_scalar_prefetch=2, grid=(t // tr,),
        in_specs=[pl.BlockSpec((tr, w), lambda i, p1, p2: (i, 0)), pl.BlockSpec(memory_space=pl.ANY)],
        out_specs=pl.BlockSpec(memory_space=pl.ANY),
        scratch_shapes=[pltpu.SemaphoreType.DMA(())])
    return pl.pallas_call(
        functools.partial(_dispatch_kernel, tr=tr),
        out_shape=jax.ShapeDtypeStruct((n_rows, w), hp.dtype), grid_spec=grid_spec,
        input_output_aliases={3: 0},
        compiler_params=_cparams("arbitrary"), name="dispatch",
    )(pos1, pos2, hp, jnp.zeros((n_rows, w), hp.dtype))


def _unpack(u):
    lo = lax.bitcast_convert_type(u << 16, F32).astype(BF16)
    hi = lax.bitcast_convert_type(u & jnp.uint32(0xFFFF0000), F32).astype(BF16)
    return lo, hi


def _new_expert(te_ref, i):
    return jnp.logical_or(i == 0, te_ref[i] != te_ref[jnp.maximum(i - 1, 0)])


def _moe_up_kernel(te_ref, nv_ref, xs_ref, w1_ref, w3_ref, o_ref, w1b_ref, w3b_ref):
    i = pl.program_id(1)
    live = i < nv_ref[0]

    @pl.when(jnp.logical_and(live, _new_expert(te_ref, i)))
    def _():
        w1b_ref[...] = w1_ref[...].astype(BF16)
        w3b_ref[...] = w3_ref[...].astype(BF16)

    @pl.when(live)
    def _():
        lo, hi = _unpack(xs_ref[...])
        half = lo.shape[1]
        a = _dot(lo, w1b_ref[0:half, :]) + _dot(hi, w1b_ref[half:2 * half, :])
        b = _dot(lo, w3b_ref[0:half, :]) + _dot(hi, w3b_ref[half:2 * half, :])
        o_ref[...] = (_silu(a) * b).astype(o_ref.dtype)

    @pl.when(jnp.logical_not(live))
    def _():
        o_ref[...] = jnp.zeros(o_ref.shape, o_ref.dtype)


def _moe_down_kernel(te_ref, nv_ref, a_ref, w_ref, o_ref, wb_ref):
    i = pl.program_id(1)
    live = i < nv_ref[0]

    @pl.when(jnp.logical_and(live, _new_expert(te_ref, i)))
    def _():
        wb_ref[...] = w_ref[...].astype(BF16)

    @pl.when(live)
    def _():
        o_ref[...] = _dot(a_ref[...], wb_ref[...])

    @pl.when(jnp.logical_not(live))
    def _():
        o_ref[...] = jnp.zeros(o_ref.shape, o_ref.dtype)


def _moe_up_f32_kernel(te_ref, nv_ref, xs_ref, w1_ref, w3_ref, o_ref):
    @pl.when(pl.program_id(1) < nv_ref[0])
    def _():
        x = xs_ref[...]
        o_ref[...] = _silu(_dot(x, w1_ref[...])) * _dot(x, w3_ref[...])

    @pl.when(pl.program_id(1) >= nv_ref[0])
    def _():
        o_ref[...] = jnp.zeros(o_ref.shape, o_ref.dtype)


def _moe_down_f32_kernel(te_ref, nv_ref, a_ref, w_ref, o_ref):
    @pl.when(pl.program_id(1) < nv_ref[0])
    def _():
        o_ref[...] = _dot(a_ref[...], w_ref[...])

    @pl.when(pl.program_id(1) >= nv_ref[0])
    def _():
        o_ref[...] = jnp.zeros(o_ref.shape, o_ref.dtype)


def _moe_experts(xs, tile_expert, n_valid, w1, w3, w2, tm, li):
    n_rows, xw = xs.shape
    _, _, d, f = w1.shape
    packed = xs.dtype == U32
    n_tiles = n_rows // tm
    tn = _tile(f, 512)
    live = lambda i, nv: jnp.minimum(i, nv[0] - 1)
    up = pl.pallas_call(
        _moe_up_kernel if packed else _moe_up_f32_kernel,
        out_shape=jax.ShapeDtypeStruct((n_rows, f), BF16 if packed else F32),
        grid_spec=pltpu.PrefetchScalarGridSpec(
            num_scalar_prefetch=2, grid=(f // tn, n_tiles),
            in_specs=[pl.BlockSpec((tm, xw), lambda j, i, te, nv: (live(i, nv), 0)),
                      pl.BlockSpec((None, None, d, tn), lambda j, i, te, nv: (li, te[live(i, nv)], 0, j)),
                      pl.BlockSpec((None, None, d, tn), lambda j, i, te, nv: (li, te[live(i, nv)], 0, j))],
            out_specs=pl.BlockSpec((tm, tn), lambda j, i, te, nv: (i, j)),
            scratch_shapes=[pltpu.VMEM((d, tn), BF16), pltpu.VMEM((d, tn), BF16)] if packed else []),
        compiler_params=_cparams("parallel", "arbitrary"), name="moe_up",
    )(tile_expert, n_valid, xs, w1, w3)
    split = 2 if tm % (4 * SUBLANES) == 0 and tm >= 256 else 1
    tm2 = tm // split
    te2 = jnp.repeat(tile_expert, split)
    nv2 = n_valid * split
    tn2 = _tile(d, 512)
    return pl.pallas_call(
        _moe_down_kernel if packed else _moe_down_f32_kernel,
        out_shape=jax.ShapeDtypeStruct((n_rows, d), F32),
        grid_spec=pltpu.PrefetchScalarGridSpec(
            num_scalar_prefetch=2, grid=(d // tn2, n_tiles * split),
            in_specs=[pl.BlockSpec((tm2, f), lambda j, i, te, nv: (live(i, nv), 0)),
                      pl.BlockSpec((None, None, f, tn2), lambda j, i, te, nv: (li, te[live(i, nv)], 0, j))],
            out_specs=pl.BlockSpec((tm2, tn2), lambda j, i, te, nv: (i, j)),
            scratch_shapes=[pltpu.VMEM((f, tn2), BF16)] if packed else []),
        compiler_params=_cparams("parallel", "arbitrary"), name="moe_down",
    )(te2, nv2, up, w2)


def _combine_kernel(p1_ref, p2_ref, x_ref, gates_ref, gmod_ref, y_ref, o_ref, buf1, buf2, sem, *, tr):
    base = pl.program_id(0) * tr

    def copy(src, buf, r):
        return pltpu.make_async_copy(y_ref.at[pl.ds(src, 1), :], buf.at[pl.ds(r, 1), :], sem)

    def issue(r, c):
        copy(p1_ref[base + r], buf1, r).start()
        copy(p2_ref[base + r], buf2, r).start()
        return c

    def drain(r, c):
        copy(p1_ref[base + r], buf1, r).wait()
        copy(p2_ref[base + r], buf2, r).wait()
        return c

    lax.fori_loop(0, tr, issue, 0)
    lax.fori_loop(0, tr, drain, 0)
    g = gates_ref[...]
    f = g[:, 0:1] * buf1[...] + g[:, 1:2] * buf2[...]
    o_ref[...] = x_ref[...] + gmod_ref[...] * f


def _combine(x, gates, gmod, y, pos1, pos2):
    t, d = x.shape
    tr = _tile(t, GATHER_TILE)
    gspec = (pl.BlockSpec((1, d), lambda i, p1, p2: (0, 0)) if gmod.shape[0] == 1
             else pl.BlockSpec((tr, d), lambda i, p1, p2: (i, 0)))
    grid_spec = pltpu.PrefetchScalarGridSpec(
        num_scalar_prefetch=2, grid=(t // tr,),
        in_specs=[pl.BlockSpec((tr, d), lambda i, p1, p2: (i, 0)),
                  pl.BlockSpec((tr, LANES), lambda i, p1, p2: (i, 0)),
                  gspec, pl.BlockSpec(memory_space=pl.ANY)],
        out_specs=pl.BlockSpec((tr, d), lambda i, p1, p2: (i, 0)),
        scratch_shapes=[pltpu.VMEM((tr, d), F32), pltpu.VMEM((tr, d), F32), pltpu.SemaphoreType.DMA(())])
    return pl.pallas_call(
        functools.partial(_combine_kernel, tr=tr),
        out_shape=jax.ShapeDtypeStruct((t, d), F32), grid_spec=grid_spec,
        compiler_params=_cparams("arbitrary"), name="combine",
    )(pos1, pos2, x, gates, gmod, y)


def _moe(x, g, sc, sh, gmod, router_w, w1, w3, w2, li, packed):
    t, d = x.shape
    n_experts = router_w.shape[1]
    tm = min(EXPERT_TILE, max(SUBLANES * 2, (2 * t) // n_experts))
    n_tiles = -(-2 * t // tm) + n_experts
    hp, sel, gates, totals = _router(x, g, sc, sh, router_w, packed)
    counts = totals[0, :n_experts].astype(I32)
    tiles_per = (counts + tm - 1) // tm
    tile_end = jnp.cumsum(tiles_per)
    offsets = jnp.pad(((tile_end - tiles_per) * tm).astype(F32), (0, LANES - n_experts)).reshape(1, LANES)
    tile_expert = jnp.minimum(jnp.sum(jnp.arange(n_tiles, dtype=I32)[:, None] >= tile_end[None, :], axis=1),
                              n_experts - 1).astype(I32)
    n_valid = tile_end[n_experts - 1:].astype(I32)
    pos = _route(sel, offsets)
    pos1, pos2 = pos[:, 0], pos[:, 1]
    xs = _dispatch(hp, pos1, pos2, n_tiles * tm)
    y = _moe_experts(xs, tile_expert, n_valid, w1, w3, w2, tm, li)
    return _combine(x, gates, gmod, y, pos1, pos2)


def _rope_tables(pos):
    half = CHUNK // 2
    inv = ROPE_THETA ** (-jnp.arange(half, dtype=F32) / half)
    ang = pos.astype(F32)[:, None] * inv[None, :]
    cos, sin = jnp.cos(ang), jnp.sin(ang)
    reps = LANES // CHUNK
    return jnp.tile(cos, (1, 2 * reps)), jnp.tile(jnp.concatenate([-sin, sin], axis=1), (1, reps))


def _lambda_init(layer):
    return 0.8 - 0.6 * math.exp(-0.3 * layer)


def _layout(d_attn, d_ssd, gn, n_heads_b, d_conv):
    src = {"q": 0, "k": d_attn, "v": 2 * d_attn, "z": 3 * d_attn, "xs": 3 * d_attn + d_ssd,
           "bm": 3 * d_attn + 2 * d_ssd, "cm": 3 * d_attn + 2 * d_ssd + gn,
           "dt": 3 * d_attn + 2 * d_ssd + 2 * gn, "ga": 3 * d_attn + 2 * d_ssd + 2 * gn + n_heads_b,
           "gg": 3 * d_attn + 2 * d_ssd + 2 * gn + n_heads_b + d_conv}
    width = {"q": d_attn, "k": d_attn, "v": d_attn, "z": d_ssd, "xs": d_ssd, "bm": gn, "cm": gn,
             "dt": n_heads_b, "ga": d_conv, "gg": d_conv}
    order = ["q", "k", "v", "xs", "bm", "cm", "z", "ga", "gg", "dt"]
    cols, off = {}, 0
    for name in order:
        cols[name] = off
        off += LANES if name == "dt" else width[name]
    tn = min(1024, d_attn)
    total = -(-off // tn) * tn
    return src, width, order, cols, total, tn


def _permute_w_in(w, src, width, order, total):
    parts = []
    for name in order:
        blk = w[:, src[name]:src[name] + width[name]]
        if name == "dt":
            blk = jnp.pad(blk, ((0, 0), (0, LANES - width[name])))
        parts.append(blk)
    out = jnp.concatenate(parts, axis=1)
    return jnp.pad(out, ((0, 0), (0, total - out.shape[1])))


def kernel(x_prompt, x_sample, c_prompt, c_sample, cache_k, cache_v, state_ssm, state_conv_ssd, state_conv_conf,
           w_ada, b_ada, norm_mix, norm_ffn, w_in, w_out, attn_lambda, attn_subln, ssd_conv_w, ssd_conv_b,
           ssd_dt_bias, ssd_a_log, ssd_d, ssd_norm, conf_dw_w, conf_dw_b, conf_ln_g, conf_ln_b,
           ffn_w1, ffn_w3, ffn_w2, moe_router, moe_w1, moe_w3, moe_w2, norm_final):
    bp, lp, d = x_prompt.shape
    bs, ls, _ = x_sample.shape
    depth = w_in.shape[0]
    past = cache_k.shape[2]
    n_heads_a, head_dim = cache_k.shape[3], cache_k.shape[5]
    assert 2 * head_dim == LANES and head_dim == CHUNK and bp == 1
    d_attn = n_heads_a * 2 * head_dim
    n_heads_b, p_dim, n_state = state_ssm.shape[2:]
    d_ssd = n_heads_b * p_dim
    conv_dim_b = state_conv_ssd.shape[3]
    gn = (conv_dim_b - d_ssd) // 2
    n_groups = gn // n_state
    d_conv = state_conv_conf.shape[3]
    ssd_dims = (n_heads_b, p_dim, n_state, n_groups, ssd_conv_w.shape[1])
    src, width, order, cols, n_total, tn_proj = _layout(d_attn, d_ssd, gn, n_heads_b, d_conv)
    n_rope_blocks = 2 * d_attn // tn_proj

    tp, ts = bp * lp, bs * ls
    pos_p = jnp.arange(lp, dtype=I32)
    pos_s = past + jnp.arange(ls, dtype=I32)
    cos_p, sin_p = _rope_tables(pos_p)
    cos_s, sin_s = _rope_tables(jnp.tile(pos_s, bs))

    c_all = jnp.concatenate([c_prompt, c_sample], axis=0)
    c_rows = -(-c_all.shape[0] // (2 * SUBLANES)) * (2 * SUBLANES)
    mods = _adaln(jnp.pad(c_all, ((0, c_rows - c_all.shape[0]), (0, 0))), w_ada, b_ada)

    xp = x_prompt.reshape(tp, d)
    xs = x_sample.reshape(ts, d)
    zeros_ssm = jnp.zeros((bp, n_heads_b, p_dim, n_state), F32)
    zeros_cb = jnp.zeros((bp, ssd_conv_w.shape[1] - 1, conv_dim_b), F32)
    zeros_cc = jnp.zeros((bp, conf_dw_w.shape[1] - 1, d_conv), F32)
    caches = (jnp.transpose(cache_k, (0, 1, 3, 4, 5, 2)).reshape(depth, bs, d_attn, past),
              cache_v.reshape(depth, bs, past, d_attn))
    outs_p, outs_s = [], []

    for l in range(depth):
        lam0 = _lambda_init(l)
        w_in_f = _permute_w_in(w_in[l], src, width, order, n_total)
        weights = {BF16: (w_in_f.astype(BF16), w_out[l].astype(BF16)), F32: (w_in_f, w_out[l])}
        mod_p = [mods[l, 0:bp, j * d:(j + 1) * d] for j in range(6)]
        mod_s = [jnp.repeat(mods[l, bp:bp + bs, j * d:(j + 1) * d], ls, axis=0) for j in range(6)]
        if l % 2 == 0:
            ffn_f = (ffn_w1[l // 2], ffn_w3[l // 2], ffn_w2[l // 2])
            ffn_ws = {BF16: tuple(w.astype(BF16) for w in ffn_f), F32: ffn_f}

        def run(x, nb, ln, mod, cos, sin, caches, ssm0, conv_b0, conv_c0):
            sh1, sc1, g1, sh2, sc2, g2 = mod
            mm = BF16 if caches is None else F32
            w_in_l, w_out_l = weights[mm]
            h = _normmod(x, norm_mix[l], sc1, sh1, mm)
            if caches is None:
                proj, k_b, k_t, v_t = _proj(h, w_in_l, cos, sin, tn_proj, n_rope_blocks, (cols["k"], cols["v"]))
                proj3 = proj.reshape(nb, ln, n_total)
                ya = _flash_attention(proj, k_b, v_t, cols["q"], n_heads_a, attn_lambda[l], attn_subln[l], lam0)
                k_new = jnp.transpose(k_t.reshape(n_heads_a, 2, head_dim, nb, ln), (3, 4, 0, 1, 2))
            else:
                proj = _proj(h, w_in_l, cos, sin, tn_proj, n_rope_blocks)
                proj3 = proj.reshape(nb, ln, n_total)
                ya = _cached_attention(proj3, cols["q"], cols["k"], cols["v"], n_heads_a, caches[0], caches[1], l,
                                       attn_lambda[l], attn_subln[l], lam0).reshape(nb * ln, d_attn)
                k_new = proj3[:, :, cols["k"]:cols["k"] + d_attn].reshape(nb, ln, n_heads_a, 2, head_dim)
            yb, ssm_new, conv_b_new = _ssd(proj3, cols, ssd_dims, conv_b0, ssd_conv_w[l], ssd_conv_b[l],
                                           ssd_dt_bias[l], ssd_a_log[l], ssd_d[l], ssd_norm[l], ssm0, mm)
            yc, conv_c_new = _conf(proj3, cols["ga"], cols["gg"], d_conv, conv_c0, conf_dw_w[l], conf_dw_b[l],
                                   conf_ln_g[l], conf_ln_b[l], mm)
            x = _outproj(ya, yb.reshape(nb * ln, d_ssd), yc.reshape(nb * ln, d_conv), w_out_l, x, g1)
            if l % 2 == 0:
                h2 = _normmod(x, norm_ffn[l], sc2, sh2, mm)
                x = _ffn(h2, *ffn_ws[mm], x, g2)
            else:
                x = _moe(x, norm_ffn[l], sc2, sh2, g2, moe_router[l // 2], moe_w1, moe_w3, moe_w2, l // 2,
                         packed=True)
            v_new = proj3[:, :, cols["v"]:cols["v"] + d_attn].reshape(nb, ln, n_heads_a, 2 * head_dim)
            return x, (k_new, v_new, ssm_new, conv_b_new, conv_c_new)

        xp, st_p = run(xp, bp, lp, mod_p, cos_p, sin_p, None, zeros_ssm, zeros_cb, zeros_cc)
        xs, st_s = run(xs, bs, ls, mod_s, cos_s, sin_s, caches, state_ssm[l], state_conv_ssd[l], state_conv_conf[l])
        outs_p.append(st_p)
        outs_s.append(st_s)

    y_prompt = _normmod(xp, norm_final, None, None, F32).reshape(bp, lp, d)
    y_sample = _normmod(xs, norm_final, None, None, F32).reshape(bs, ls, d)
    stack = lambda outs, k: jnp.stack([o[k] for o in outs])
    return (y_prompt, y_sample,
            stack(outs_p, 0), stack(outs_p, 1), stack(outs_p, 2), stack(outs_p, 3), stack(outs_p, 4),
            stack(outs_s, 0), stack(outs_s, 1), stack(outs_s, 2), stack(outs_s, 3), stack(outs_s, 4))
```

```python
import functools
import math

import numpy as np
import jax
import jax.numpy as jnp
from jax import lax
from jax.experimental import pallas as pl
from jax.experimental.pallas import tpu as pltpu

F32 = jnp.float32
BF16 = jnp.bfloat16
U32 = jnp.uint32
I32 = jnp.int32

CHUNK = 64
ROPE_THETA = 10000.0
EPS = 1e-6

LANES = 128
SUBLANES = 8
VMEM_LIMIT_BYTES = 48 * 1024 * 1024

ROW_TILE = 512
ATTN_TQ = 2048
ATTN_TK = 1024
ONES_ROWS = 16
LOG2_E = 1.4426950408889634
SSD_CHUNK = 128
CONF_TILE = 256
EXPERT_TILE = 512
GATHER_TILE = 256


def _cparams(*sem):
    return pltpu.CompilerParams(dimension_semantics=sem, vmem_limit_bytes=VMEM_LIMIT_BYTES)


def _tile(n, pref):
    t = min(n, pref)
    assert n % t == 0, (n, pref)
    return t


def _rows_spec(arr, tm, width, by_col=False):
    if arr.shape[0] == 1:
        return pl.BlockSpec((1, width), lambda j, i: (0, j if by_col else 0))
    return pl.BlockSpec((tm, width), lambda j, i: (i, j if by_col else 0))


def _silu(x):
    return x * jax.nn.sigmoid(x)


def _precision(a, b):
    assert a.dtype == b.dtype, (a.dtype, b.dtype)
    return lax.Precision.HIGHEST if a.dtype == F32 else None


def _dot(a, b):
    return jnp.dot(a, b, preferred_element_type=F32, precision=_precision(a, b))


def _dot_nt(a, b):
    return lax.dot_general(a, b, (((1,), (1,)), ((), ())), preferred_element_type=F32, precision=_precision(a, b))


def _adaln_kernel(c_ref, w_ref, b_ref, o_ref):
    o_ref[0] = _dot(_silu(c_ref[...]), w_ref[0]) + b_ref[0]


def _adaln(c_all, w_ada, b_ada):
    depth, d, mc = w_ada.shape
    rows = c_all.shape[0]
    tn = _tile(mc, 1024)
    return pl.pallas_call(
        _adaln_kernel,
        out_shape=jax.ShapeDtypeStruct((depth, rows, mc), F32),
        grid=(depth, mc // tn),
        in_specs=[pl.BlockSpec((rows, d), lambda l, j: (0, 0)),
                  pl.BlockSpec((1, d, tn), lambda l, j: (l, 0, j)),
                  pl.BlockSpec((1, 1, tn), lambda l, j: (l, 0, j))],
        out_specs=pl.BlockSpec((1, rows, tn), lambda l, j: (l, 0, j)),
        compiler_params=_cparams("parallel", "parallel"),
        name="adaln",
    )(c_all, w_ada, b_ada.reshape(depth, 1, mc))


def _modulated(x, g, sc, sh):
    ms = jnp.mean(x * x, axis=-1, keepdims=True)
    y = x * lax.rsqrt(ms + EPS) * g
    if sc is not None:
        y = y * (1.0 + sc) + sh
    return y


def _normmod_kernel(x_ref, g_ref, sc_ref, sh_ref, o_ref):
    o_ref[...] = _modulated(x_ref[...], g_ref[...], sc_ref[...], sh_ref[...]).astype(o_ref.dtype)


def _norm_kernel(x_ref, g_ref, o_ref):
    o_ref[...] = _modulated(x_ref[...], g_ref[...], None, None).astype(o_ref.dtype)


def _normmod(x, g, sc, sh, out_dtype):
    t, d = x.shape
    tm = _tile(t, ROW_TILE)
    g = g.reshape(1, d)
    row = pl.BlockSpec((tm, d), lambda j, i: (i, 0))
    vec = pl.BlockSpec((1, d), lambda j, i: (0, 0))
    if sc is None:
        body, ops, specs = _norm_kernel, (x, g), [row, vec]
    else:
        body, ops = _normmod_kernel, (x, g, sc, sh)
        specs = [row, vec, _rows_spec(sc, tm, d), _rows_spec(sh, tm, d)]
    return pl.pallas_call(
        body, out_shape=jax.ShapeDtypeStruct((t, d), out_dtype),
        grid=(1, t // tm), in_specs=specs, out_specs=row,
        compiler_params=_cparams("parallel", "parallel"), name="normmod",
    )(*ops)


def _proj_kernel(a_ref, w_ref, cos_ref, sin_ref, o_ref, *extra_refs, n_rope_blocks, jk, jv):
    acc = _dot(a_ref[...], w_ref[...])
    j = pl.program_id(1)

    @pl.when(j >= n_rope_blocks)
    def _():
        o_ref[...] = acc

    @pl.when(j < n_rope_blocks)
    def _():
        cos = cos_ref[...]
        sin = sin_ref[...]
        lane = lax.broadcasted_iota(I32, cos.shape, 1)
        first_half = (lane % CHUNK) < (CHUNK // 2)
        for c in range(acc.shape[1] // LANES):
            blk = acc[:, c * LANES:(c + 1) * LANES]
            partner = jnp.where(first_half, pltpu.roll(blk, LANES - 32, 1), pltpu.roll(blk, 32, 1))
            o_ref[:, c * LANES:(c + 1) * LANES] = blk * cos + partner * sin

    if extra_refs:
        kb_ref, kt_ref, vt_ref = extra_refs

        @pl.when(j == jk)
        def _():
            k = o_ref[...]
            kb_ref[...] = k.astype(BF16)
            kt_ref[...] = k.T

        @pl.when(j == jv)
        def _():
            vt_ref[...] = acc.T.astype(BF16)


def _proj(h, w, cos, sin, tn, n_rope_blocks, extras=None):
    t, d = h.shape
    n = w.shape[1]
    tm = _tile(t, ROW_TILE)
    out_shape = [jax.ShapeDtypeStruct((t, n), F32)]
    out_specs = [pl.BlockSpec((tm, tn), lambda i, j: (i, j))]
    jk = jv = -1
    if extras is not None:
        assert extras[0] % tn == 0 and extras[1] % tn == 0
        jk, jv = extras[0] // tn, extras[1] // tn
        out_shape += [jax.ShapeDtypeStruct((t, tn), BF16), jax.ShapeDtypeStruct((tn, t), F32),
                      jax.ShapeDtypeStruct((tn, t), BF16)]
        out_specs += [pl.BlockSpec((tm, tn), lambda i, j: (i, 0)), pl.BlockSpec((tn, tm), lambda i, j: (0, i)),
                      pl.BlockSpec((tn, tm), lambda i, j: (0, i))]
    outs = pl.pallas_call(
        functools.partial(_proj_kernel, n_rope_blocks=n_rope_blocks, jk=jk, jv=jv),
        out_shape=out_shape,
        grid=(t // tm, n // tn),
        in_specs=[pl.BlockSpec((tm, d), lambda i, j: (i, 0)),
                  pl.BlockSpec((d, tn), lambda i, j: (0, j)),
                  pl.BlockSpec((tm, LANES), lambda i, j: (i, 0)),
                  pl.BlockSpec((tm, LANES), lambda i, j: (i, 0))],
        out_specs=out_specs,
        compiler_params=_cparams("parallel", "arbitrary"), name="proj",
    )(h, w, cos, sin)
    return outs if extras is not None else outs[0]


def _lambda_value(lam_ref, lambda_init):
    lp = lam_ref[...]
    a = jnp.sum(lp[0:1] * lp[1:2], axis=-1, keepdims=True)
    b = jnp.sum(lp[2:3] * lp[3:4], axis=-1, keepdims=True)
    return jnp.exp(a) - jnp.exp(b) + lambda_init


def _stack_maps(q):
    lane = lax.broadcasted_iota(I32, q.shape, 1)
    lo = lane < (LANES // 2)
    return jnp.concatenate([jnp.where(lo, q, 0.0), jnp.where(lo, 0.0, q)], axis=0)


def _diff_finish(o1, o2, lam, g, lambda_init):
    o = o1 - lam * o2
    ms = jnp.mean(o * o, axis=-1, keepdims=True)
    return o * lax.rsqrt(ms + EPS) * g * (1.0 - lambda_init)


def _flash_kernel(qi_ref, ki_ref, q_ref, k_ref, vtp_ref, vt_ref, lam_ref, g_ref, o_ref,
                  qst_ref, m_ref, acc_ref, pp_ref, ap_ref, *, tq, tk, lambda_init):
    p = pl.program_id(1)
    qi = qi_ref[p]
    ki = ki_ref[p]
    ones = jnp.ones((ONES_ROWS, tk), BF16)

    @pl.when(ki == 0)
    def _():
        qt = (q_ref[...] * (float(CHUNK) ** -0.5 * LOG2_E)).T
        lo = lax.broadcasted_iota(I32, qt.shape, 0) < (LANES // 2)
        qst_ref[:, 0:tq] = jnp.where(lo, qt, 0.0).astype(BF16)
        qst_ref[:, tq:2 * tq] = jnp.where(lo, 0.0, qt).astype(BF16)
        m_ref[...] = jnp.full(m_ref.shape, -jnp.inf, F32)
        acc_ref[...] = jnp.zeros(acc_ref.shape, F32)
        pp_ref[...] = jnp.zeros(pp_ref.shape, BF16)
        ap_ref[...] = jnp.zeros(ap_ref.shape, F32)

    def step(masked):
        st = _dot(k_ref[...], qst_ref[...])
        if masked:
            k_chunk = (ki * tk + lax.broadcasted_iota(I32, st.shape, 0)) // CHUNK
            q_chunk = (qi * tq + lax.broadcasted_iota(I32, st.shape, 1) % tq) // CHUNK
            st = jnp.where(k_chunk <= q_chunk, st, -jnp.inf)
        prev = _dot(jnp.concatenate([vtp_ref[...], ones], axis=0), pp_ref[...])
        acc_ref[...] = ap_ref[...] * acc_ref[...] + prev
        m_prev = m_ref[...]
        m_new = jnp.maximum(m_prev, jnp.max(st, axis=0, keepdims=True))
        ap_ref[...] = jnp.exp2(m_prev - m_new)
        pp_ref[...] = jnp.exp2((st - m_new).astype(BF16))
        m_ref[...] = m_new

    needs_mask = (ki + 1) * tk > qi * tq

    @pl.when(jnp.logical_not(needs_mask))
    def _():
        step(False)

    @pl.when(needs_mask)
    def _():
        step(True)

    @pl.when((ki + 1) * tk == (qi + 1) * tq)
    def _():
        last = _dot(jnp.concatenate([vt_ref[...], ones], axis=0), pp_ref[...])
        acc = ap_ref[...] * acc_ref[...] + last
        o = acc[0:LANES, :] * (1.0 / acc[LANES:LANES + 1, :])
        lam = _lambda_value(lam_ref, lambda_init)
        o = (o[:, 0:tq] - lam * o[:, tq:2 * tq]).T
        ms = jnp.mean(o * o, axis=-1, keepdims=True)
        o_ref[...] = (o * lax.rsqrt(ms + EPS) * g_ref[...] * (1.0 - lambda_init)).astype(o_ref.dtype)


def _flash_attention(proj, k_b, v_t, col_q, n_heads, lam_p, subln, lambda_init):
    t = proj.shape[0]
    tq = _tile(t, ATTN_TQ)
    tk = _tile(tq, ATTN_TK)
    assert tk % CHUNK == 0
    per_q = tq // tk
    pairs = [(a, b) for a in range(t // tq) for b in range((a + 1) * per_q)]
    qi = jnp.asarray(np.array([a for a, _ in pairs], np.int32))
    ki = jnp.asarray(np.array([b for _, b in pairs], np.int32))
    cq = col_q // LANES
    grid_spec = pltpu.PrefetchScalarGridSpec(
        num_scalar_prefetch=2,
        grid=(n_heads, len(pairs)),
        in_specs=[pl.BlockSpec((tq, LANES), lambda h, p, qi, ki: (qi[p], cq + h)),
                  pl.BlockSpec((tk, LANES), lambda h, p, qi, ki: (ki[p], h)),
                  pl.BlockSpec((LANES, tk), lambda h, p, qi, ki: (h, jnp.maximum(ki[p] - 1, 0))),
                  pl.BlockSpec((LANES, tk), lambda h, p, qi, ki: (h, ki[p])),
                  pl.BlockSpec(lam_p.shape, lambda h, p, qi, ki: (0, 0)),
                  pl.BlockSpec((1, LANES), lambda h, p, qi, ki: (0, 0))],
        out_specs=pl.BlockSpec((tq, LANES), lambda h, p, qi, ki: (qi[p], h)),
        scratch_shapes=[pltpu.VMEM((LANES, 2 * tq), BF16),
                        pltpu.VMEM((1, 2 * tq), F32),
                        pltpu.VMEM((LANES + ONES_ROWS, 2 * tq), F32),
                        pltpu.VMEM((tk, 2 * tq), BF16),
                        pltpu.VMEM((1, 2 * tq), F32)])
    return pl.pallas_call(
        functools.partial(_flash_kernel, tq=tq, tk=tk, lambda_init=lambda_init),
        out_shape=jax.ShapeDtypeStruct((t, n_heads * LANES), BF16),
        grid_spec=grid_spec,
        compiler_params=_cparams("parallel", "arbitrary"), name="flash_attn",
    )(qi, ki, proj, k_b, v_t, v_t, lam_p, subln.reshape(1, LANES))


def _cached_attn_kernel(q_ref, kn_ref, vn_ref, kct_ref, vc_ref, lam_ref, g_ref, o_ref, *, lambda_init):
    ln = q_ref.shape[0]
    past = kct_ref.shape[1]
    qs = _stack_maps(q_ref[...] * (float(CHUNK) ** -0.5))
    sc = _dot(qs, kct_ref[...])
    sn = _dot_nt(qs, kn_ref[...])
    q_chunk_c = (past + lax.broadcasted_iota(I32, sc.shape, 0) % ln) // CHUNK
    q_chunk_n = (past + lax.broadcasted_iota(I32, sn.shape, 0) % ln) // CHUNK
    sc = jnp.where(lax.broadcasted_iota(I32, sc.shape, 1) // CHUNK <= q_chunk_c, sc, -jnp.inf)
    sn = jnp.where((past + lax.broadcasted_iota(I32, sn.shape, 1)) // CHUNK <= q_chunk_n, sn, -jnp.inf)
    m = jnp.maximum(jnp.max(sc, axis=1, keepdims=True), jnp.max(sn, axis=1, keepdims=True))
    pc = jnp.exp(sc - m)
    pn = jnp.exp(sn - m)
    denom = jnp.sum(pc, axis=1, keepdims=True) + jnp.sum(pn, axis=1, keepdims=True)
    o = (_dot(pc, vc_ref[...]) + _dot(pn, vn_ref[...])) / denom
    lam = _lambda_value(lam_ref, lambda_init)
    o_ref[...] = _diff_finish(o[0:ln], o[ln:2 * ln], lam, g_ref[...], lambda_init).astype(o_ref.dtype)


def _cached_attention(proj3, col_q, col_k, col_v, n_heads, k_cache_t, v_cache, li, lam_p, subln, lambda_init):
    b, ln, _ = proj3.shape
    past = v_cache.shape[2]
    cq, ck, cv = col_q // LANES, col_k // LANES, col_v // LANES
    new = lambda c: pl.BlockSpec((None, ln, LANES), lambda bi, h: (bi, 0, c + h))
    return pl.pallas_call(
        functools.partial(_cached_attn_kernel, lambda_init=lambda_init),
        out_shape=jax.ShapeDtypeStruct((b, ln, n_heads * LANES), F32),
        grid=(b, n_heads),
        in_specs=[new(cq), new(ck), new(cv),
                  pl.BlockSpec((None, None, LANES, past), lambda bi, h: (li, bi, h, 0)),
                  pl.BlockSpec((None, None, past, LANES), lambda bi, h: (li, bi, 0, h)),
                  pl.BlockSpec(lam_p.shape, lambda bi, h: (0, 0)),
                  pl.BlockSpec((1, LANES), lambda bi, h: (0, 0))],
        out_specs=pl.BlockSpec((None, ln, LANES), lambda bi, h: (bi, 0, h)),
        compiler_params=_cparams("parallel", "parallel"), name="cached_attn",
    )(proj3, proj3, proj3, k_cache_t, v_cache, lam_p, subln.reshape(1, LANES))


def _split3(x):
    x1 = x.astype(BF16)
    r1 = x - x1.astype(F32)
    x2 = r1.astype(BF16)
    x3 = (r1 - x2.astype(F32)).astype(BF16)
    return x1, x2, x3


def _ssd_kernel(xs_ref, bm_ref, cm_ref, hx_ref, hb_ref, hc_ref, sx_ref, sb_ref, sc_ref,
                wx_ref, wb_ref, wc_ref, bx_ref, bb_ref, bc_ref,
                dt_ref, dtb_ref, alog_ref, z_ref, dexp_ref, gn_ref, h0_ref,
                y_ref, hout_ref, nsx_ref, nsb_ref, nsc_ref,
                scr_x, scr_b, scr_c, h_scr, y_scr, xte_scr,
                *, q, qp, n_heads, p_dim, n_state, n_groups, width):
    i = pl.program_id(1)
    first = i == 0
    halo = SUBLANES

    @pl.when(first)
    def _():
        h_scr[...] = h0_ref[...]

    def conv(t_ref, halo_ref, st_ref, w_ref, b_ref, scr, ns_ref):
        scr[0:halo] = jnp.where(first, st_ref[...], halo_ref[...])
        scr[halo:halo + q] = t_ref[...]
        acc = b_ref[...] + w_ref[0:1, :] * scr[halo - width + 1:halo - width + 1 + q]
        for k in range(1, width):
            acc = acc + w_ref[k:k + 1, :] * scr[halo - width + 1 + k:halo - width + 1 + k + q]
        ns_ref[...] = scr[q:q + halo]
        return _silu(acc)

    xs = conv(xs_ref, hx_ref, sx_ref, wx_ref, bx_ref, scr_x, nsx_ref)
    bm = conv(bm_ref, hb_ref, sb_ref, wb_ref, bb_ref, scr_b, nsb_ref)
    cm = conv(cm_ref, hc_ref, sc_ref, wc_ref, bc_ref, scr_c, nsc_ref)
    dt = jax.nn.softplus(dt_ref[...] + dtb_ref[...])
    a = -jnp.exp(alog_ref[...])

    def pad(v):
        if qp == q:
            return v
        return jnp.concatenate([v, jnp.zeros((qp - q, v.shape[1]), v.dtype)], axis=0)

    xs_p, bm_p, cm_p, dt_p = pad(xs), pad(bm), pad(cm), pad(dt)
    d_a = dt_p * a
    row = lax.broadcasted_iota(I32, (qp, qp), 0)
    col = lax.broadcasted_iota(I32, (qp, qp), 1)
    tril = row >= col
    ones_tril = jnp.where(tril, 1.0, 0.0).astype(BF16)
    d1, d2, d3 = _split3(d_a)
    acum = _dot(ones_tril, d1) + _dot(ones_tril, d2) + _dot(ones_tril, d3)
    acum_t = acum.T
    total = acum[qp - 1:qp, :]
    mm = y_ref.dtype
    bm_b = bm_p.astype(mm)
    cm_b = cm_p.astype(mm)
    scores = [_dot_nt(cm_b[:, g * n_state:(g + 1) * n_state], bm_b[:, g * n_state:(g + 1) * n_state])
              for g in range(n_groups)]
    rep = n_heads // n_groups
    for h in range(n_heads):
        g = h // rep
        colv = acum[:, h:h + 1]
        rowv = acum_t[h:h + 1, :]
        tot = total[:, h:h + 1]
        decay = jnp.where(tril, jnp.exp(colv - rowv), 0.0)
        xdt = xs_p[:, h * p_dim:(h + 1) * p_dim] * dt_p[:, h:h + 1]
        y_diag = _dot((scores[g] * decay).astype(mm), xdt.astype(mm))
        h_prev = h_scr[h]
        y_off = jnp.exp(colv) * _dot_nt(cm_b[:, g * n_state:(g + 1) * n_state], h_prev.astype(mm))
        y_scr[:, h * p_dim:(h + 1) * p_dim] = y_diag + y_off
        xte_scr[:, h * p_dim:(h + 1) * p_dim] = xdt * jnp.exp(tot - colv)
    xte_t = xte_scr[...].T.astype(mm)
    for h in range(n_heads):
        g = h // rep
        st = _dot(xte_t[h * p_dim:(h + 1) * p_dim, :], bm_b[:, g * n_state:(g + 1) * n_state])
        h_scr[h] = jnp.exp(total[:, h:h + 1]) * h_scr[h] + st
    y = y_scr[0:q] + dexp_ref[...] * xs
    yg = y * _silu(z_ref[...])
    ms = jnp.mean(yg * yg, axis=-1, keepdims=True)
    y_ref[...] = (yg * lax.rsqrt(ms + EPS) * gn_ref[...]).astype(y_ref.dtype)
    hout_ref[...] = h_scr[...]


def _ssd(proj3, cols, dims, conv_state, conv_w, conv_b, dt_bias, a_log, d_skip, norm_g, h0, out_dtype):
    b, ln, _ = proj3.shape
    n_heads, p_dim, n_state, n_groups, width = dims
    d_ssd = n_heads * p_dim
    gn = n_groups * n_state
    q = _tile(ln, SSD_CHUNK)
    qp = max(q, LANES)
    assert q % SUBLANES == 0 and ln >= width - 1 and width - 1 <= SUBLANES and n_heads <= LANES
    nsteps = ln // q
    halo = SUBLANES
    widths = (d_ssd, gn, gn)
    offs = (cols["xs"], cols["bm"], cols["cm"])
    for o, w in zip(offs, widths):
        assert o % w == 0
    assert cols["z"] % d_ssd == 0 and cols["dt"] % LANES == 0

    def tile_spec(o, w):
        return pl.BlockSpec((None, q, w), lambda bi, i: (bi, i, o // w))

    def halo_spec(o, w):
        return pl.BlockSpec((None, halo, w), lambda bi, i: (bi, jnp.maximum(i * (q // halo) - 1, 0), o // w))

    def state_spec(w):
        return pl.BlockSpec((None, halo, w), lambda bi, i: (bi, 0, 0))

    def const_spec(shape):
        return pl.BlockSpec(shape, lambda bi, i: (0,) * len(shape))

    st = jnp.pad(conv_state, ((0, 0), (halo - (width - 1), 0), (0, 0)))
    bounds = (0, d_ssd, d_ssd + gn, d_ssd + 2 * gn)
    st_segs = [st[..., bounds[k]:bounds[k + 1]] for k in range(3)]
    w_segs = [conv_w[:, bounds[k]:bounds[k + 1]] for k in range(3)]
    b_segs = [conv_b[bounds[k]:bounds[k + 1]].reshape(1, -1) for k in range(3)]
    pad_lane = lambda v: jnp.pad(v.reshape(1, -1), ((0, 0), (0, LANES - v.shape[0])))
    in_specs = ([tile_spec(o, w) for o, w in zip(offs, widths)]
                + [halo_spec(o, w) for o, w in zip(offs, widths)]
                + [state_spec(w) for w in widths]
                + [const_spec((width, w)) for w in widths]
                + [const_spec((1, w)) for w in widths]
                + [tile_spec(cols["dt"], LANES), const_spec((1, LANES)), const_spec((1, LANES)),
                   tile_spec(cols["z"], d_ssd), const_spec((1, d_ssd)), const_spec((1, d_ssd)),
                   pl.BlockSpec((None, n_heads, p_dim, n_state), lambda bi, i: (bi, 0, 0, 0))])
    out_shape = (jax.ShapeDtypeStruct((b, ln, d_ssd), out_dtype),
                 jax.ShapeDtypeStruct((b, n_heads, p_dim, n_state), F32),
                 jax.ShapeDtypeStruct((b, halo, d_ssd), F32),
                 jax.ShapeDtypeStruct((b, halo, gn), F32),
                 jax.ShapeDtypeStruct((b, halo, gn), F32))
    out_specs = (pl.BlockSpec((None, q, d_ssd), lambda bi, i: (bi, i, 0)),
                 pl.BlockSpec((None, n_heads, p_dim, n_state), lambda bi, i: (bi, 0, 0, 0)),
                 state_spec(d_ssd), state_spec(gn), state_spec(gn))
    scratch = [pltpu.VMEM((q + halo, d_ssd), F32), pltpu.VMEM((q + halo, gn), F32), pltpu.VMEM((q + halo, gn), F32),
               pltpu.VMEM((n_heads, p_dim, n_state), F32),
               pltpu.VMEM((qp, d_ssd), F32), pltpu.VMEM((qp, d_ssd), F32)]
    y, h_new, nsx, nsb, nsc = pl.pallas_call(
        functools.partial(_ssd_kernel, q=q, qp=qp, n_heads=n_heads, p_dim=p_dim, n_state=n_state,
                          n_groups=n_groups, width=width),
        out_shape=out_shape, grid=(b, nsteps), in_specs=in_specs, out_specs=out_specs,
        scratch_shapes=scratch,
        compiler_params=_cparams("parallel", "arbitrary"), name="ssd",
    )(proj3, proj3, proj3, proj3, proj3, proj3, *st_segs, *w_segs, *b_segs,
      proj3, pad_lane(dt_bias), pad_lane(a_log), proj3,
      jnp.repeat(d_skip, p_dim).reshape(1, d_ssd), norm_g.reshape(1, d_ssd), h0)
    new_state = jnp.concatenate([nsx, nsb, nsc], axis=-1)[:, halo - (width - 1):, :]
    return y, h_new, new_state


def _conf_kernel(ga_ref, gg_ref, hga_ref, hgg_ref, st_ref, w_ref, b_ref, lg_ref, lb_ref,
                 y_ref, ns_ref, scr, *, tb, width, halo):
    i = pl.program_id(1)
    u = ga_ref[...] * jax.nn.sigmoid(gg_ref[...])
    uh = hga_ref[...] * jax.nn.sigmoid(hgg_ref[...])
    scr[0:halo] = jnp.where(i == 0, st_ref[...], uh)
    scr[halo:halo + tb] = u
    base = halo - (width - 1)
    acc = b_ref[...] + w_ref[0:1, :] * scr[base:base + tb]
    for k in range(1, width):
        acc = acc + w_ref[k:k + 1, :] * scr[base + k:base + k + tb]
    mu = jnp.mean(acc, axis=-1, keepdims=True)
    cen = acc - mu
    var = jnp.mean(cen * cen, axis=-1, keepdims=True)
    y = cen * lax.rsqrt(var + EPS) * lg_ref[...] + lb_ref[...]
    y_ref[...] = _silu(y).astype(y_ref.dtype)
    ns_ref[...] = scr[tb:tb + halo]


def _conf(proj3, col_ga, col_gg, d_conv, conv_state, w, bias, ln_g, ln_b, out_dtype):
    b, ln, _ = proj3.shape
    width = w.shape[0]
    halo = -(-(width - 1) // SUBLANES) * SUBLANES
    tb = _tile(ln, CONF_TILE)
    assert tb % halo == 0 and ln >= width - 1 and col_ga % d_conv == 0 and col_gg % d_conv == 0
    st = jnp.pad(conv_state, ((0, 0), (halo - (width - 1), 0), (0, 0)))
    tile_spec = lambda o: pl.BlockSpec((None, tb, d_conv), lambda bi, i: (bi, i, o // d_conv))
    halo_spec = lambda o: pl.BlockSpec((None, halo, d_conv),
                                       lambda bi, i: (bi, jnp.maximum(i * (tb // halo) - 1, 0), o // d_conv))
    st_spec = pl.BlockSpec((None, halo, d_conv), lambda bi, i: (bi, 0, 0))
    vec = pl.BlockSpec((1, d_conv), lambda bi, i: (0, 0))
    y, ns = pl.pallas_call(
        functools.partial(_conf_kernel, tb=tb, width=width, halo=halo),
        out_shape=(jax.ShapeDtypeStruct((b, ln, d_conv), out_dtype), jax.ShapeDtypeStruct((b, halo, d_conv), F32)),
        grid=(b, ln // tb),
        in_specs=[tile_spec(col_ga), tile_spec(col_gg), halo_spec(col_ga), halo_spec(col_gg), st_spec,
                  pl.BlockSpec((width, d_conv), lambda bi, i: (0, 0)), vec, vec, vec],
        out_specs=(pl.BlockSpec((None, tb, d_conv), lambda bi, i: (bi, i, 0)), st_spec),
        scratch_shapes=[pltpu.VMEM((tb + halo, d_conv), F32)],
        compiler_params=_cparams("parallel", "arbitrary"), name="conf_conv",
    )(proj3, proj3, proj3, proj3, st, w, bias.reshape(1, -1), ln_g.reshape(1, -1), ln_b.reshape(1, -1))
    return y, ns[:, halo - (width - 1):, :]


def _outproj_kernel(ya_ref, yb_ref, yc_ref, wa_ref, wb_ref, wc_ref, res_ref, gate_ref, o_ref):
    acc = _dot(ya_ref[...], wa_ref[...]) + _dot(yb_ref[...], wb_ref[...]) + _dot(yc_ref[...], wc_ref[...])
    o_ref[...] = res_ref[...] + gate_ref[...] * acc


def _outproj(ya, yb, yc, w_out, res, gate):
    t, d = res.shape
    da, db = ya.shape[1], yb.shape[1]
    tm = _tile(t, ROW_TILE)
    tn = _tile(d, 1024)
    a_spec = lambda k: pl.BlockSpec((tm, k), lambda j, i: (i, 0))
    w_spec = lambda k: pl.BlockSpec((k, tn), lambda j, i: (0, j))
    return pl.pallas_call(
        _outproj_kernel, out_shape=jax.ShapeDtypeStruct((t, d), F32),
        grid=(d // tn, t // tm),
        in_specs=[a_spec(da), a_spec(db), a_spec(yc.shape[1]), w_spec(da), w_spec(db), w_spec(yc.shape[1]),
                  pl.BlockSpec((tm, tn), lambda j, i: (i, j)), _rows_spec(gate, tm, tn, by_col=True)],
        out_specs=pl.BlockSpec((tm, tn), lambda j, i: (i, j)),
        compiler_params=_cparams("parallel", "parallel"), name="outproj",
    )(ya, yb, yc, w_out[:da], w_out[da:da + db], w_out[da + db:], res, gate)


def _ffn_up_kernel(a_ref, w1_ref, w3_ref, o_ref):
    a = a_ref[...]
    o_ref[...] = (_silu(_dot(a, w1_ref[...])) * _dot(a, w3_ref[...])).astype(o_ref.dtype)


def _ffn_down_kernel(a_ref, w_ref, res_ref, gate_ref, o_ref):
    o_ref[...] = res_ref[...] + gate_ref[...] * _dot(a_ref[...], w_ref[...])


def _ffn(h, w1, w3, w2, res, gate):
    t, d = h.shape
    f = w1.shape[1]
    tm = _tile(t, ROW_TILE)
    tn = _tile(f, 512)
    up = pl.pallas_call(
        _ffn_up_kernel, out_shape=jax.ShapeDtypeStruct((t, f), h.dtype),
        grid=(f // tn, t // tm),
        in_specs=[pl.BlockSpec((tm, d), lambda j, i: (i, 0)),
                  pl.BlockSpec((d, tn), lambda j, i: (0, j)),
                  pl.BlockSpec((d, tn), lambda j, i: (0, j))],
        out_specs=pl.BlockSpec((tm, tn), lambda j, i: (i, j)),
        compiler_params=_cparams("parallel", "parallel"), name="ffn_up",
    )(h, w1, w3)
    tn2 = _tile(d, 512)
    return pl.pallas_call(
        _ffn_down_kernel, out_shape=jax.ShapeDtypeStruct((t, d), F32),
        grid=(d // tn2, t // tm),
        in_specs=[pl.BlockSpec((tm, f), lambda j, i: (i, 0)),
                  pl.BlockSpec((f, tn2), lambda j, i: (0, j)),
                  pl.BlockSpec((tm, tn2), lambda j, i: (i, j)),
                  _rows_spec(gate, tm, tn2, by_col=True)],
        out_specs=pl.BlockSpec((tm, tn2), lambda j, i: (i, j)),
        compiler_params=_cparams("parallel", "parallel"), name="ffn_down",
    )(up, w2, res, gate)


def _router_kernel(x_ref, g_ref, sc_ref, sh_ref, r_ref, hp_ref, sel_ref, gate_ref, tot_ref, *, n_experts):
    h = _modulated(x_ref[...], g_ref[...], sc_ref[...], sh_ref[...])
    half = h.shape[1] // 2
    lo = lax.bitcast_convert_type(h[:, :half].astype(BF16).astype(F32), U32)
    hi = lax.bitcast_convert_type(h[:, half:].astype(BF16).astype(F32), U32)
    hp_ref[...] = (lo >> 16) | hi
    r = r_ref[...]
    h1 = h.astype(BF16)
    h2 = (h - h1.astype(F32)).astype(BF16)
    r1 = r.astype(BF16)
    r2 = (r - r1.astype(F32)).astype(BF16)
    logits = _dot(h1, r1) + _dot(h1, r2) + _dot(h2, r1)
    lane = lax.broadcasted_iota(I32, logits.shape, 1)
    lane_f = lane.astype(F32)
    lg = jnp.where(lane < n_experts, logits, -jnp.inf)
    v1 = jnp.max(lg, axis=1, keepdims=True)
    i1 = jnp.min(jnp.where(lg == v1, lane_f, float(LANES)), axis=1, keepdims=True)
    lg2 = jnp.where(lane_f == i1, -jnp.inf, lg)
    v2 = jnp.max(lg2, axis=1, keepdims=True)
    i2 = jnp.min(jnp.where(lg2 == v2, lane_f, float(LANES)), axis=1, keepdims=True)
    e = jnp.exp(v2 - v1)
    g1 = 1.0 / (1.0 + e)
    g2 = e * g1
    first = lane_f == i1
    second = lane_f == i2
    sel_ref[...] = jnp.where(first, 1, jnp.where(second, 2, 0)).astype(I32)
    gate_ref[...] = jnp.where(lane == 0, g1, jnp.where(lane == 1, g2, 0.0))
    cnt = jnp.sum(jnp.where(first | second, 1.0, 0.0), axis=0, keepdims=True)

    @pl.when(pl.program_id(1) == 0)
    def _():
        tot_ref[...] = jnp.zeros(tot_ref.shape, F32)

    tot_ref[...] += cnt


def _router(x, g, sc, sh, router_w):
    t, d = x.shape
    n_experts = router_w.shape[1]
    tm = _tile(t, ROW_TILE)
    r = jnp.pad(router_w, ((0, 0), (0, LANES - n_experts)))
    row = lambda w: pl.BlockSpec((tm, w), lambda j, i: (i, 0))
    hw, hdt = d // 2, U32
    return pl.pallas_call(
        functools.partial(_router_kernel, n_experts=n_experts),
        out_shape=(jax.ShapeDtypeStruct((t, hw), hdt), jax.ShapeDtypeStruct((t, LANES), I32),
                   jax.ShapeDtypeStruct((t, LANES), F32), jax.ShapeDtypeStruct((1, LANES), F32)),
        grid=(1, t // tm),
        in_specs=[row(d), pl.BlockSpec((1, d), lambda j, i: (0, 0)), _rows_spec(sc, tm, d), _rows_spec(sh, tm, d),
                  pl.BlockSpec((d, LANES), lambda j, i: (0, 0))],
        out_specs=(row(hw), row(LANES), row(LANES), pl.BlockSpec((1, LANES), lambda j, i: (0, 0))),
        compiler_params=_cparams("arbitrary", "arbitrary"), name="router",
    )(x, g.reshape(1, d), sc, sh, r)


def _route_kernel(sel_ref, off_ref, pos_ref, carry_ref):
    @pl.when(pl.program_id(0) == 0)
    def _():
        carry_ref[...] = jnp.zeros(carry_ref.shape, F32)

    s = sel_ref[...]
    tm = s.shape[0]
    cnt = jnp.where(s > 0, 1.0, 0.0)
    row = lax.broadcasted_iota(I32, (tm, tm), 0)
    col = lax.broadcasted_iota(I32, (tm, tm), 1)
    before = jnp.where(row > col, 1.0, 0.0).astype(BF16)
    dest = _dot(before, cnt.astype(BF16)) + carry_ref[...] + off_ref[...]
    p1 = jnp.sum(jnp.where(s == 1, dest, 0.0), axis=1, keepdims=True)
    p2 = jnp.sum(jnp.where(s == 2, dest, 0.0), axis=1, keepdims=True)
    lane = lax.broadcasted_iota(I32, s.shape, 1)
    pos_ref[...] = jnp.where(lane == 0, p1, jnp.where(lane == 1, p2, 0.0)).astype(I32)
    carry_ref[...] += jnp.sum(cnt, axis=0, keepdims=True)


def _route(sel, offsets):
    t = sel.shape[0]
    tm = _tile(t, ROW_TILE)
    return pl.pallas_call(
        _route_kernel, out_shape=jax.ShapeDtypeStruct((t, LANES), I32),
        grid=(t // tm,),
        in_specs=[pl.BlockSpec((tm, LANES), lambda i: (i, 0)), pl.BlockSpec((1, LANES), lambda i: (0, 0))],
        out_specs=pl.BlockSpec((tm, LANES), lambda i: (i, 0)),
        scratch_shapes=[pltpu.VMEM((1, LANES), F32)],
        compiler_params=_cparams("arbitrary"), name="route",
    )(sel, offsets)


def _dispatch_kernel(p1_ref, p2_ref, hp_ref, init_ref, xs_ref, sem, *, tr):
    del init_ref
    base = pl.program_id(0) * tr

    def copy(r, dst):
        return pltpu.make_async_copy(hp_ref.at[pl.ds(r, 1), :], xs_ref.at[pl.ds(dst, 1), :], sem)

    def issue(r, c):
        copy(r, p1_ref[base + r]).start()
        copy(r, p2_ref[base + r]).start()
        return c

    def drain(r, c):
        copy(r, p1_ref[base + r]).wait()
        copy(r, p2_ref[base + r]).wait()
        return c

    lax.fori_loop(0, tr, issue, 0)
    lax.fori_loop(0, tr, drain, 0)


def _dispatch(hp, pos1, pos2, n_rows):
    t, w = hp.shape
    tr = _tile(t, GATHER_TILE)
    grid_spec = pltpu.PrefetchScalarGridSpec(
        num_scalar_prefetch=2, grid=(t // tr,),
        in_specs=[pl.BlockSpec((tr, w), lambda i, p1, p2: (i, 0)), pl.BlockSpec(memory_space=pl.ANY)],
        out_specs=pl.BlockSpec(memory_space=pl.ANY),
        scratch_shapes=[pltpu.SemaphoreType.DMA(())])
    return pl.pallas_call(
        functools.partial(_dispatch_kernel, tr=tr),
        out_shape=jax.ShapeDtypeStruct((n_rows, w), hp.dtype), grid_spec=grid_spec,
        input_output_aliases={3: 0},
        compiler_params=_cparams("arbitrary"), name="dispatch",
    )(pos1, pos2, hp, jnp.zeros((n_rows, w), hp.dtype))


def _unpack(u):
    lo = lax.bitcast_convert_type(u << 16, F32).astype(BF16)
    hi = lax.bitcast_convert_type(u & jnp.uint32(0xFFFF0000), F32).astype(BF16)
    return lo, hi


def _new_expert(te_ref, i):
    return jnp.logical_or(i == 0, te_ref[i] != te_ref[jnp.maximum(i - 1, 0)])


def _moe_up_kernel(te_ref, nv_ref, xs_ref, w1_ref, w3_ref, o_ref, w1b_ref, w3b_ref):
    i = pl.program_id(1)
    live = i < nv_ref[0]

    @pl.when(jnp.logical_and(live, _new_expert(te_ref, i)))
    def _():
        w1b_ref[...] = w1_ref[...].astype(BF16)
        w3b_ref[...] = w3_ref[...].astype(BF16)

    @pl.when(live)
    def _():
        lo, hi = _unpack(xs_ref[...])
        half = lo.shape[1]
        a = _dot(lo, w1b_ref[0:half, :]) + _dot(hi, w1b_ref[half:2 * half, :])
        b = _dot(lo, w3b_ref[0:half, :]) + _dot(hi, w3b_ref[half:2 * half, :])
        o_ref[...] = (_silu(a) * b).astype(o_ref.dtype)

    @pl.when(jnp.logical_not(live))
    def _():
        o_ref[...] = jnp.zeros(o_ref.shape, o_ref.dtype)


def _moe_down_kernel(te_ref, nv_ref, a_ref, w_ref, o_ref, wb_ref):
    i = pl.program_id(1)
    live = i < nv_ref[0]

    @pl.when(jnp.logical_and(live, _new_expert(te_ref, i)))
    def _():
        wb_ref[...] = w_ref[...].astype(BF16)

    @pl.when(live)
    def _():
        o_ref[...] = _dot(a_ref[...], wb_ref[...])

    @pl.when(jnp.logical_not(live))
    def _():
        o_ref[...] = jnp.zeros(o_ref.shape, o_ref.dtype)


def _moe_experts(xs, tile_expert, n_valid, w1, w3, w2, tm, li):
    n_rows, xw = xs.shape
    _, _, d, f = w1.shape
    assert xs.dtype == U32
    n_tiles = n_rows // tm
    tn = _tile(f, 512)
    live = lambda i, nv: jnp.minimum(i, nv[0] - 1)
    up = pl.pallas_call(
        _moe_up_kernel, out_shape=jax.ShapeDtypeStruct((n_rows, f), BF16),
        grid_spec=pltpu.PrefetchScalarGridSpec(
            num_scalar_prefetch=2, grid=(f // tn, n_tiles),
            in_specs=[pl.BlockSpec((tm, xw), lambda j, i, te, nv: (live(i, nv), 0)),
                      pl.BlockSpec((None, None, d, tn), lambda j, i, te, nv: (li, te[live(i, nv)], 0, j)),
                      pl.BlockSpec((None, None, d, tn), lambda j, i, te, nv: (li, te[live(i, nv)], 0, j))],
            out_specs=pl.BlockSpec((tm, tn), lambda j, i, te, nv: (i, j)),
            scratch_shapes=[pltpu.VMEM((d, tn), BF16), pltpu.VMEM((d, tn), BF16)]),
        compiler_params=_cparams("parallel", "arbitrary"), name="moe_up",
    )(tile_expert, n_valid, xs, w1, w3)
    split = 2 if tm % (4 * SUBLANES) == 0 and tm >= 256 else 1
    tm2 = tm // split
    te2 = jnp.repeat(tile_expert, split)
    nv2 = n_valid * split
    tn2 = _tile(d, 512)
    return pl.pallas_call(
        _moe_down_kernel, out_shape=jax.ShapeDtypeStruct((n_rows, d), F32),
        grid_spec=pltpu.PrefetchScalarGridSpec(
            num_scalar_prefetch=2, grid=(d // tn2, n_tiles * split),
            in_specs=[pl.BlockSpec((tm2, f), lambda j, i, te, nv: (live(i, nv), 0)),
                      pl.BlockSpec((None, None, f, tn2), lambda j, i, te, nv: (li, te[live(i, nv)], 0, j))],
            out_specs=pl.BlockSpec((tm2, tn2), lambda j, i, te, nv: (i, j)),
            scratch_shapes=[pltpu.VMEM((f, tn2), BF16)]),
        compiler_params=_cparams("parallel", "arbitrary"), name="moe_down",
    )(te2, nv2, up, w2)


def _combine_kernel(p1_ref, p2_ref, x_ref, gates_ref, gmod_ref, y_ref, o_ref, buf1, buf2, sem, *, tr):
    base = pl.program_id(0) * tr

    def copy(src, buf, r):
        return pltpu.make_async_copy(y_ref.at[pl.ds(src, 1), :], buf.at[pl.ds(r, 1), :], sem)

    def issue(r, c):
        copy(p1_ref[base + r], buf1, r).start()
        copy(p2_ref[base + r], buf2, r).start()
        return c

    def drain(r, c):
        copy(p1_ref[base + r], buf1, r).wait()
        copy(p2_ref[base + r], buf2, r).wait()
        return c

    lax.fori_loop(0, tr, issue, 0)
    lax.fori_loop(0, tr, drain, 0)
    g = gates_ref[...]
    f = g[:, 0:1] * buf1[...] + g[:, 1:2] * buf2[...]
    o_ref[...] = x_ref[...] + gmod_ref[...] * f


def _combine(x, gates, gmod, y, pos1, pos2):
    t, d = x.shape
    tr = _tile(t, GATHER_TILE)
    gspec = (pl.BlockSpec((1, d), lambda i, p1, p2: (0, 0)) if gmod.shape[0] == 1
             else pl.BlockSpec((tr, d), lambda i, p1, p2: (i, 0)))
    grid_spec = pltpu.PrefetchScalarGridSpec(
        num_scalar_prefetch=2, grid=(t // tr,),
        in_specs=[pl.BlockSpec((tr, d), lambda i, p1, p2: (i, 0)),
                  pl.BlockSpec((tr, LANES), lambda i, p1, p2: (i, 0)),
                  gspec, pl.BlockSpec(memory_space=pl.ANY)],
        out_specs=pl.BlockSpec((tr, d), lambda i, p1, p2: (i, 0)),
        scratch_shapes=[pltpu.VMEM((tr, d), F32), pltpu.VMEM((tr, d), F32), pltpu.SemaphoreType.DMA(())])
    return pl.pallas_call(
        functools.partial(_combine_kernel, tr=tr),
        out_shape=jax.ShapeDtypeStruct((t, d), F32), grid_spec=grid_spec,
        compiler_params=_cparams("arbitrary"), name="combine",
    )(pos1, pos2, x, gates, gmod, y)


def _moe(x, g, sc, sh, gmod, router_w, w1, w3, w2, li):
    t, d = x.shape
    n_experts = router_w.shape[1]
    tm = min(EXPERT_TILE, max(SUBLANES * 2, (2 * t) // n_experts))
    n_tiles = -(-2 * t // tm) + n_experts
    hp, sel, gates, totals = _router(x, g, sc, sh, router_w)
    counts = totals[0, :n_experts].astype(I32)
    tiles_per = (counts + tm - 1) // tm
    tile_end = jnp.cumsum(tiles_per)
    offsets = jnp.pad(((tile_end - tiles_per) * tm).astype(F32), (0, LANES - n_experts)).reshape(1, LANES)
    tile_expert = jnp.minimum(jnp.sum(jnp.arange(n_tiles, dtype=I32)[:, None] >= tile_end[None, :], axis=1),
                              n_experts - 1).astype(I32)
    n_valid = tile_end[n_experts - 1:].astype(I32)
    pos = _route(sel, offsets)
    pos1, pos2 = pos[:, 0], pos[:, 1]
    xs = _dispatch(hp, pos1, pos2, n_tiles * tm)
    y = _moe_experts(xs, tile_expert, n_valid, w1, w3, w2, tm, li)
    return _combine(x, gates, gmod, y, pos1, pos2)


def _rope_tables(pos):
    half = CHUNK // 2
    inv = ROPE_THETA ** (-jnp.arange(half, dtype=F32) / half)
    ang = pos.astype(F32)[:, None] * inv[None, :]
    cos, sin = jnp.cos(ang), jnp.sin(ang)
    reps = LANES // CHUNK
    return jnp.tile(cos, (1, 2 * reps)), jnp.tile(jnp.concatenate([-sin, sin], axis=1), (1, reps))


def _lambda_init(layer):
    return 0.8 - 0.6 * math.exp(-0.3 * layer)


def _layout(d_attn, d_ssd, gn, n_heads_b, d_conv):
    src = {"q": 0, "k": d_attn, "v": 2 * d_attn, "z": 3 * d_attn, "xs": 3 * d_attn + d_ssd,
           "bm": 3 * d_attn + 2 * d_ssd, "cm": 3 * d_attn + 2 * d_ssd + gn,
           "dt": 3 * d_attn + 2 * d_ssd + 2 * gn, "ga": 3 * d_attn + 2 * d_ssd + 2 * gn + n_heads_b,
           "gg": 3 * d_attn + 2 * d_ssd + 2 * gn + n_heads_b + d_conv}
    width = {"q": d_attn, "k": d_attn, "v": d_attn, "z": d_ssd, "xs": d_ssd, "bm": gn, "cm": gn,
             "dt": n_heads_b, "ga": d_conv, "gg": d_conv}
    order = ["q", "k", "v", "xs", "bm", "cm", "z", "ga", "gg", "dt"]
    cols, off = {}, 0
    for name in order:
        cols[name] = off
        off += LANES if name == "dt" else width[name]
    tn = min(1024, d_attn)
    total = -(-off // tn) * tn
    return src, width, order, cols, total, tn


def _permute_w_in(w, src, width, order, total):
    parts = []
    for name in order:
        blk = w[:, src[name]:src[name] + width[name]]
        if name == "dt":
            blk = jnp.pad(blk, ((0, 0), (0, LANES - width[name])))
        parts.append(blk)
    out = jnp.concatenate(parts, axis=1)
    return jnp.pad(out, ((0, 0), (0, total - out.shape[1])))


def kernel(x_prompt, x_sample, c_prompt, c_sample, cache_k, cache_v, state_ssm, state_conv_ssd, state_conv_conf,
           w_ada, b_ada, norm_mix, norm_ffn, w_in, w_out, attn_lambda, attn_subln, ssd_conv_w, ssd_conv_b,
           ssd_dt_bias, ssd_a_log, ssd_d, ssd_norm, conf_dw_w, conf_dw_b, conf_ln_g, conf_ln_b,
           ffn_w1, ffn_w3, ffn_w2, moe_router, moe_w1, moe_w3, moe_w2, norm_final):
    bp, lp, d = x_prompt.shape
    bs, ls, _ = x_sample.shape
    depth = w_in.shape[0]
    past = cache_k.shape[2]
    n_heads_a, head_dim = cache_k.shape[3], cache_k.shape[5]
    assert 2 * head_dim == LANES and head_dim == CHUNK and bp == 1
    d_attn = n_heads_a * 2 * head_dim
    n_heads_b, p_dim, n_state = state_ssm.shape[2:]
    d_ssd = n_heads_b * p_dim
    conv_dim_b = state_conv_ssd.shape[3]
    gn = (conv_dim_b - d_ssd) // 2
    n_groups = gn // n_state
    d_conv = state_conv_conf.shape[3]
    ssd_dims = (n_heads_b, p_dim, n_state, n_groups, ssd_conv_w.shape[1])
    src, width, order, cols, n_total, tn_proj = _layout(d_attn, d_ssd, gn, n_heads_b, d_conv)
    n_rope_blocks = 2 * d_attn // tn_proj

    tp, ts = bp * lp, bs * ls
    pos_p = jnp.arange(lp, dtype=I32)
    pos_s = past + jnp.arange(ls, dtype=I32)
    cos_p, sin_p = _rope_tables(pos_p)
    cos_s, sin_s = _rope_tables(jnp.tile(pos_s, bs))

    c_all = jnp.concatenate([c_prompt, c_sample], axis=0)
    c_rows = -(-c_all.shape[0] // (2 * SUBLANES)) * (2 * SUBLANES)
    mods = _adaln(jnp.pad(c_all, ((0, c_rows - c_all.shape[0]), (0, 0))), w_ada, b_ada)

    xp = x_prompt.reshape(tp, d)
    xs = x_sample.reshape(ts, d)
    zeros_ssm = jnp.zeros((bp, n_heads_b, p_dim, n_state), F32)
    zeros_cb = jnp.zeros((bp, ssd_conv_w.shape[1] - 1, conv_dim_b), F32)
    zeros_cc = jnp.zeros((bp, conf_dw_w.shape[1] - 1, d_conv), F32)
    caches = (jnp.transpose(cache_k, (0, 1, 3, 4, 5, 2)).reshape(depth, bs, d_attn, past),
              cache_v.reshape(depth, bs, past, d_attn))
    outs_p, outs_s = [], []

    for l in range(depth):
        lam0 = _lambda_init(l)
        w_in_f = _permute_w_in(w_in[l], src, width, order, n_total)
        weights = {BF16: (w_in_f.astype(BF16), w_out[l].astype(BF16)), F32: (w_in_f, w_out[l])}
        mod_p = [mods[l, 0:bp, j * d:(j + 1) * d] for j in range(6)]
        mod_s = [jnp.repeat(mods[l, bp:bp + bs, j * d:(j + 1) * d], ls, axis=0) for j in range(6)]
        if l % 2 == 0:
            ffn_f = (ffn_w1[l // 2], ffn_w3[l // 2], ffn_w2[l // 2])
            ffn_ws = {BF16: tuple(w.astype(BF16) for w in ffn_f), F32: ffn_f}

        def run(x, nb, ln, mod, cos, sin, caches, ssm0, conv_b0, conv_c0):
            sh1, sc1, g1, sh2, sc2, g2 = mod
            mm = BF16 if caches is None else F32
            w_in_l, w_out_l = weights[mm]
            h = _normmod(x, norm_mix[l], sc1, sh1, mm)
            if caches is None:
                proj, k_b, k_t, v_t = _proj(h, w_in_l, cos, sin, tn_proj, n_rope_blocks, (cols["k"], cols["v"]))
                proj3 = proj.reshape(nb, ln, n_total)
                ya = _flash_attention(proj, k_b, v_t, cols["q"], n_heads_a, attn_lambda[l], attn_subln[l], lam0)
                k_new = jnp.transpose(k_t.reshape(n_heads_a, 2, head_dim, nb, ln), (3, 4, 0, 1, 2))
            else:
                proj = _proj(h, w_in_l, cos, sin, tn_proj, n_rope_blocks)
                proj3 = proj.reshape(nb, ln, n_total)
                ya = _cached_attention(proj3, cols["q"], cols["k"], cols["v"], n_heads_a, caches[0], caches[1], l,
                                       attn_lambda[l], attn_subln[l], lam0).reshape(nb * ln, d_attn)
                k_new = proj3[:, :, cols["k"]:cols["k"] + d_attn].reshape(nb, ln, n_heads_a, 2, head_dim)
            yb, ssm_new, conv_b_new = _ssd(proj3, cols, ssd_dims, conv_b0, ssd_conv_w[l], ssd_conv_b[l],
                                           ssd_dt_bias[l], ssd_a_log[l], ssd_d[l], ssd_norm[l], ssm0, mm)
            yc, conv_c_new = _conf(proj3, cols["ga"], cols["gg"], d_conv, conv_c0, conf_dw_w[l], conf_dw_b[l],
                                   conf_ln_g[l], conf_ln_b[l], mm)
            x = _outproj(ya, yb.reshape(nb * ln, d_ssd), yc.reshape(nb * ln, d_conv), w_out_l, x, g1)
            if l % 2 == 0:
                h2 = _normmod(x, norm_ffn[l], sc2, sh2, mm)
                x = _ffn(h2, *ffn_ws[mm], x, g2)
            else:
                x = _moe(x, norm_ffn[l], sc2, sh2, g2, moe_router[l // 2], moe_w1, moe_w3, moe_w2, l // 2)
            v_new = proj3[:, :, cols["v"]:cols["v"] + d_attn].reshape(nb, ln, n_heads_a, 2 * head_dim)
            return x, (k_new, v_new, ssm_new, conv_b_new, conv_c_new)

        xp, st_p = run(xp, bp, lp, mod_p, cos_p, sin_p, None, zeros_ssm, zeros_cb, zeros_cc)
        xs, st_s = run(xs, bs, ls, mod_s, cos_s, sin_s, caches, state_ssm[l], state_conv_ssd[l], state_conv_conf[l])
        outs_p.append(st_p)
        outs_s.append(st_s)

    y_prompt = _normmod(xp, norm_final, None, None, F32).reshape(bp, lp, d)
    y_sample = _normmod(xs, norm_final, None, None, F32).reshape(bs, ls, d)
    stack = lambda outs, k: jnp.stack([o[k] for o in outs])
    return (y_prompt, y_sample,
            stack(outs_p, 0), stack(outs_p, 1), stack(outs_p, 2), stack(outs_p, 3), stack(outs_p, 4),
            stack(outs_s, 0), stack(outs_s, 1), stack(outs_s, 2), stack(outs_s, 3), stack(outs_s, 4))
```
